```python
import jax
import jax.numpy as jnp
from jax import lax
import numpy as np

D_MODEL = 1024
BATCH = 8
SEQ = 8192
DEPTH = 2

RWKV_HEADS = 8
RWKV_HEAD_DIM = 64
RWKV_W = RWKV_HEADS * RWKV_HEAD_DIM
DECAY_LORA = 64
ICLR_LORA = 64
GATE_LORA = 128
VRES_LORA = 32
GN_EPS = 64e-5
GMLP_GROUPS = 4
GMLP_W = 512
GMLP_GROUP_DIM = GMLP_W // GMLP_GROUPS
CHUNK = 128
LRU_HEADS = 8
LRU_W = 512
LRU_HEAD_DIM = LRU_W // LRU_HEADS
CONV_WIDTH = 4
LRU_C = 8.0
N_BRANCH = 3
BRANCH_W = 512
D_FF = 2816
LN_EPS = 1e-5
ALPHA = (2 * DEPTH) ** 0.25
BETA = (8 * DEPTH) ** -0.25

OFF_R = 0
OFF_K = OFF_R + RWKV_W
OFF_V = OFF_K + RWKV_W
OFF_W = OFF_V + RWKV_W
OFF_A = OFF_W + DECAY_LORA
OFF_G = OFF_A + ICLR_LORA
RWKV_COLS = OFF_G + GATE_LORA
OFF_GU = RWKV_COLS
OFF_GV = OFF_GU + GMLP_W
OFF_LX = OFF_GV + GMLP_W
OFF_LY = OFF_LX + LRU_W
OFF_GATE = OFF_LY + LRU_W
N_IN = OFF_GATE + N_BRANCH * D_MODEL

kernel_name = 'hybrid_rwkv7_gmlp_rglru_deepnorm'


def layer_norm(x, g, b, eps=LN_EPS):
    xf = x.astype(jnp.float32)
    mu = jnp.mean(xf, -1, keepdims=True)
    var = jnp.mean(jnp.square(xf - mu), -1, keepdims=True)
    y = (xf - mu) * lax.rsqrt(var + eps) * g.astype(jnp.float32) + b.astype(jnp.float32)
    return y.astype(x.dtype)


def swiglu(x, w1, w3, w2):
    return (jax.nn.silu(x @ w1) * (x @ w3)) @ w2


def token_shift(z):
    return jnp.pad(z, ((0, 0), (1, 0), (0, 0)))[:, :-1]


def rwkv7_wkv(r, w, k, v, kk, a):
    B, _, H, N = r.shape
    xs = tuple(jnp.swapaxes(t, 0, 1) for t in (r, w, k, v, kk, a))

    def step(state, inp):
        r_t, w_t, k_t, v_t, kk_t, a_t = inp
        sa = jnp.einsum('bhvk,bhk->bhv', state, -kk_t)
        state = (state * w_t[:, :, None, :]
                 + sa[..., None] * (kk_t * a_t)[:, :, None, :]
                 + v_t[..., None] * k_t[:, :, None, :])
        return state, jnp.einsum('bhvk,bhk->bhv', state, r_t)

    state0 = jnp.zeros((B, H, N, N), jnp.float32)
    _, out = lax.scan(step, state0, xs)
    return jnp.swapaxes(out, 0, 1)


def rwkv7_branch(z, v_first, w0, w2, a0, a2, g2, k_k, k_a, r_k, gn_g, gn_b, vres):
    B, S, _ = z.shape
    f32 = jnp.float32
    heads = lambda t: t.reshape(B, S, RWKV_HEADS, RWKV_HEAD_DIM).astype(f32)
    r = z[..., OFF_R:OFF_K]
    k = z[..., OFF_K:OFF_V]
    v = z[..., OFF_V:OFF_W]
    zw = z[..., OFF_W:OFF_A]
    za = z[..., OFF_A:OFF_G]
    zg = z[..., OFF_G:RWKV_COLS]
    v_raw = v
    if vres is not None:
        v0, v1, v2 = vres
        v = v + (v_first - v) * jax.nn.sigmoid(v0 + (v @ v1) @ v2)
    w_log = -jax.nn.softplus(-(w0 + jnp.tanh(zw) @ w2)) - 0.5
    decay = jnp.exp(-jnp.exp(w_log.astype(f32)))
    a = jax.nn.sigmoid(a0 + za @ a2)
    g = jax.nn.sigmoid(zg) @ g2
    kk = heads(k * k_k)
    kk = kk / jnp.maximum(jnp.linalg.norm(kk, axis=-1, keepdims=True), 1e-12)
    k = k * (1.0 + (a - 1.0) * k_a)
    rh, kh, vh = heads(r), heads(k), heads(v)
    o = rwkv7_wkv(rh, heads(decay), kh, vh, kk, heads(a))
    mu = jnp.mean(o, -1, keepdims=True)
    var = jnp.mean(jnp.square(o - mu), -1, keepdims=True)
    o = ((o - mu) * lax.rsqrt(var + GN_EPS)).reshape(B, S, RWKV_W) * gn_g + gn_b
    bonus = jnp.sum(rh * kh * r_k, -1, keepdims=True) * vh
    o = (o + bonus.reshape(B, S, RWKV_W)) * g
    return o.astype(z.dtype), v_raw


def gmlp_branch(zu, zv, ln_g, ln_b, ws, sb):
    B, S, _ = zu.shape
    u = jax.nn.gelu(zu)
    v = layer_norm(jax.nn.gelu(zv), ln_g, ln_b)
    vc = v.reshape(B, S // CHUNK, CHUNK, GMLP_GROUPS, GMLP_GROUP_DIM)
    mask = jnp.tril(jnp.ones((CHUNK, CHUNK), dtype=bool))
    wm = jnp.where(mask[None], ws, jnp.zeros_like(ws))
    s = jnp.einsum('gts,bcsgd->bctgd', wm, vc) + jnp.swapaxes(sb, 0, 1)[None, None, :, :, None]
    return u * s.reshape(B, S, GMLP_W)


def rglru_branch(zx, zy, conv_w, conv_b, wa, ba, wx, bx, lam):
    B, S, _ = zx.shape
    f32 = jnp.float32
    y = jax.nn.gelu(zy)
    xp = jnp.pad(zx, ((0, 0), (CONV_WIDTH - 1, 0), (0, 0)))
    xc = conv_b + xp[:, 0:S] * conv_w[0]
    for j in range(1, CONV_WIDTH):
        xc = xc + xp[:, j:j + S] * conv_w[j]
    xh = xc.reshape(B, S, LRU_HEADS, LRU_HEAD_DIM)
    rg = jax.nn.sigmoid(jnp.einsum('bshi,hij->bshj', xh, wa) + ba).reshape(B, S, LRU_W)
    ig = jax.nn.sigmoid(jnp.einsum('bshi,hij->bshj', xh, wx) + bx).reshape(B, S, LRU_W)
    log_a = -LRU_C * rg.astype(f32) * jax.nn.softplus(-lam.astype(f32))
    a = jnp.exp(log_a)
    b = jnp.sqrt(-jnp.expm1(2.0 * log_a)) * (ig * xc).astype(f32)

    def combine(left, right):
        a_l, b_l = left
        a_r, b_r = right
        return a_l * a_r, a_r * b_l + b_r

    _, h = lax.associative_scan(combine, (a, b), axis=1)
    return y * h.astype(zx.dtype)


def hybrid_mixer(x, w_in, gate_b, p_branch, w_out, mu, v_first, rwkv_p, vres, gmlp_p, lru_p):
    B, S, _ = x.shape
    z = x @ w_in
    zr = z[..., :RWKV_COLS]
    zr = zr + mu * (token_shift(zr) - zr)
    o_rwkv, v_raw = rwkv7_branch(zr, v_first, *rwkv_p, vres)
    o_gmlp = gmlp_branch(z[..., OFF_GU:OFF_GV], z[..., OFF_GV:OFF_LX], *gmlp_p)
    o_lru = rglru_branch(z[..., OFF_LX:OFF_LY], z[..., OFF_LY:OFF_GATE], *lru_p)
    gates = jax.nn.sigmoid(z[..., OFF_GATE:].reshape(B, S, N_BRANCH, D_MODEL) + gate_b)
    merged = (gates[:, :, 0] * (o_rwkv @ p_branch[0])
              + gates[:, :, 1] * (o_gmlp @ p_branch[1])
              + gates[:, :, 2] * (o_lru @ p_branch[2]))
    return merged @ w_out, v_raw


def setup_inputs(seed: int = 0) -> dict:
    key = jax.random.key(seed)
    ks = iter(jax.random.split(key, 48))
    f32 = jnp.float32

    def nrm(shape, scale):
        return jax.random.normal(next(ks), shape, f32) * scale

    x = nrm((BATCH, SEQ, D_MODEL), 1.0)
    ln_g = 1.0 + nrm((DEPTH, 3, D_MODEL), 0.02)
    ln_b = nrm((DEPTH, 3, D_MODEL), 0.02)
    ffn_w1 = nrm((DEPTH, 2, D_MODEL, D_FF), D_MODEL ** -0.5)
    ffn_w3 = nrm((DEPTH, 2, D_MODEL, D_FF), D_MODEL ** -0.5)
    ffn_w2 = nrm((DEPTH, 2, D_FF, D_MODEL), BETA * D_FF ** -0.5)
    w_in = nrm((DEPTH, D_MODEL, N_IN), D_MODEL ** -0.5)
    gate_b = nrm((DEPTH, N_BRANCH, D_MODEL), 0.02)
    p_branch = nrm((DEPTH, N_BRANCH, BRANCH_W, D_MODEL), BRANCH_W ** -0.5)
    w_out = nrm((DEPTH, D_MODEL, D_MODEL), BETA * D_MODEL ** -0.5)
    rwkv_mu = jax.random.uniform(next(ks), (DEPTH, RWKV_COLS), f32)
    rwkv_w0 = jnp.tile(jnp.linspace(-6.0, -1.0, RWKV_HEAD_DIM, dtype=f32), RWKV_HEADS)[None, :] + nrm((DEPTH, RWKV_W), 0.1)
    rwkv_w2 = nrm((DEPTH, DECAY_LORA, RWKV_W), 0.1 * DECAY_LORA ** -0.5)
    rwkv_a0 = nrm((DEPTH, RWKV_W), 0.1)
    rwkv_a2 = nrm((DEPTH, ICLR_LORA, RWKV_W), 0.5 * ICLR_LORA ** -0.5)
    rwkv_g2 = nrm((DEPTH, GATE_LORA, RWKV_W), GATE_LORA ** -0.5)
    rwkv_k_k = 0.85 + nrm((DEPTH, RWKV_W), 0.05)
    rwkv_k_a = 1.0 + nrm((DEPTH, RWKV_W), 0.05)
    rwkv_r_k = nrm((DEPTH, RWKV_HEADS, RWKV_HEAD_DIM), 0.1)
    rwkv_gn_g = 1.0 + nrm((DEPTH, RWKV_W), 0.02)
    rwkv_gn_b = nrm((DEPTH, RWKV_W), 0.02)
    rwkv_v0 = 1.0 + nrm((DEPTH - 1, RWKV_W), 0.1)
    rwkv_v1 = nrm((DEPTH - 1, RWKV_W, VRES_LORA), RWKV_W ** -0.5)
    rwkv_v2 = nrm((DEPTH - 1, VRES_LORA, RWKV_W), 0.5 * VRES_LORA ** -0.5)
    gmlp_ln_g = 1.0 + nrm((DEPTH, GMLP_W), 0.02)
    gmlp_ln_b = nrm((DEPTH, GMLP_W), 0.02)
    gmlp_ws = nrm((DEPTH, GMLP_GROUPS, CHUNK, CHUNK), CHUNK ** -0.5)
    gmlp_sb = 1.0 + nrm((DEPTH, GMLP_GROUPS, CHUNK), 0.1)
    lru_conv_w = nrm((DEPTH, CONV_WIDTH, LRU_W), CONV_WIDTH ** -0.5)
    lru_conv_b = nrm((DEPTH, LRU_W), 0.02)
    lru_wa = nrm((DEPTH, LRU_HEADS, LRU_HEAD_DIM, LRU_HEAD_DIM), LRU_HEAD_DIM ** -0.5)
    lru_ba = nrm((DEPTH, LRU_HEADS, LRU_HEAD_DIM), 0.02)
    lru_wx = nrm((DEPTH, LRU_HEADS, LRU_HEAD_DIM, LRU_HEAD_DIM), LRU_HEAD_DIM ** -0.5)
    lru_bx = nrm((DEPTH, LRU_HEADS, LRU_HEAD_DIM), 0.02)
    u = jax.random.uniform(next(ks), (DEPTH, LRU_W), f32, 0.9, 0.999)
    s = u ** (1.0 / LRU_C)
    lru_lam = jnp.log(s) - jnp.log1p(-s)
    return {'x': x, 'ln_g': ln_g, 'ln_b': ln_b, 'ffn_w1': ffn_w1, 'ffn_w3': ffn_w3, 'ffn_w2': ffn_w2,
            'w_in': w_in, 'gate_b': gate_b, 'p_branch': p_branch, 'w_out': w_out,
            'rwkv_mu': rwkv_mu, 'rwkv_w0': rwkv_w0, 'rwkv_w2': rwkv_w2, 'rwkv_a0': rwkv_a0, 'rwkv_a2': rwkv_a2,
            'rwkv_g2': rwkv_g2, 'rwkv_k_k': rwkv_k_k, 'rwkv_k_a': rwkv_k_a, 'rwkv_r_k': rwkv_r_k,
            'rwkv_gn_g': rwkv_gn_g, 'rwkv_gn_b': rwkv_gn_b, 'rwkv_v0': rwkv_v0, 'rwkv_v1': rwkv_v1, 'rwkv_v2': rwkv_v2,
            'gmlp_ln_g': gmlp_ln_g, 'gmlp_ln_b': gmlp_ln_b, 'gmlp_ws': gmlp_ws, 'gmlp_sb': gmlp_sb,
            'lru_conv_w': lru_conv_w, 'lru_conv_b': lru_conv_b, 'lru_wa': lru_wa, 'lru_ba': lru_ba,
            'lru_wx': lru_wx, 'lru_bx': lru_bx, 'lru_lam': lru_lam}


def reference(x, ln_g, ln_b, ffn_w1, ffn_w3, ffn_w2, w_in, gate_b, p_branch, w_out,
              rwkv_mu, rwkv_w0, rwkv_w2, rwkv_a0, rwkv_a2, rwkv_g2, rwkv_k_k, rwkv_k_a, rwkv_r_k,
              rwkv_gn_g, rwkv_gn_b, rwkv_v0, rwkv_v1, rwkv_v2,
              gmlp_ln_g, gmlp_ln_b, gmlp_ws, gmlp_sb,
              lru_conv_w, lru_conv_b, lru_wa, lru_ba, lru_wx, lru_bx, lru_lam):
    v_first = None
    for l in range(DEPTH):
        x = layer_norm(ALPHA * x + 0.5 * swiglu(x, ffn_w1[l, 0], ffn_w3[l, 0], ffn_w2[l, 0]), ln_g[l, 0], ln_b[l, 0])
        vres = None if l == 0 else (rwkv_v0[l - 1], rwkv_v1[l - 1], rwkv_v2[l - 1])
        rwkv_p = (rwkv_w0[l], rwkv_w2[l], rwkv_a0[l], rwkv_a2[l], rwkv_g2[l], rwkv_k_k[l], rwkv_k_a[l],
                  rwkv_r_k[l], rwkv_gn_g[l], rwkv_gn_b[l])
        gmlp_p = (gmlp_ln_g[l], gmlp_ln_b[l], gmlp_ws[l], gmlp_sb[l])
        lru_p = (lru_conv_w[l], lru_conv_b[l], lru_wa[l], lru_ba[l], lru_wx[l], lru_bx[l], lru_lam[l])
        y, v_raw = hybrid_mixer(x, w_in[l], gate_b[l], p_branch[l], w_out[l], rwkv_mu[l], v_first,
                                rwkv_p, vres, gmlp_p, lru_p)
        if l == 0:
            v_first = v_raw
        x = layer_norm(ALPHA * x + y, ln_g[l, 1], ln_b[l, 1])
        x = layer_norm(ALPHA * x + 0.5 * swiglu(x, ffn_w1[l, 1], ffn_w3[l, 1], ffn_w2[l, 1]), ln_g[l, 2], ln_b[l, 2])
    return x
```

```python
import functools

import jax
import jax.numpy as jnp
from jax import lax
from jax.experimental import pallas as pl
from jax.experimental.pallas import tpu as pltpu

F32 = jnp.float32
BF16 = jnp.bfloat16

D_MODEL = 1024
N_BATCH = 8
D_FF = 2816
FF_CHUNK = 256
N_FF_CHUNKS = D_FF // FF_CHUNK
BRANCH_W = 512
HEAD_DIM = 64
N_HEADS = BRANCH_W // HEAD_DIM
RWKV_COLS = 1792
GMLP_GROUPS = 4
GMLP_GROUP_DIM = 128
GMLP_CHUNK = 128
WKV_CHUNK = 64
HEADS_PER_GROUP = 4
GROUP_W = HEADS_PER_GROUP * HEAD_DIM
N_GROUPS = N_HEADS // HEADS_PER_GROUP
LN_EPS = 1e-5
GN_EPS = 64e-5
LRU_C = 8.0
DEPTH = 2
ALPHA = (2 * DEPTH) ** 0.25
TOKEN_TILE = 512
VMEM_LIMIT = 56 * 1024 * 1024


def _dot(a, b):
    return jnp.dot(a, b, preferred_element_type=F32)


def _dot_nt(a, b):
    return lax.dot_general(a, b, (((1,), (1,)), ((), ())), preferred_element_type=F32)


def _dot_tn(a, b):
    return lax.dot_general(a, b, (((0,), (0,)), ((), ())), preferred_element_type=F32)


def _sigmoid(x):
    return 1.0 / (1.0 + jnp.exp(-x))


def _softplus(x):
    return jnp.maximum(x, 0.0) + jnp.log(1.0 + jnp.exp(-jnp.abs(x)))


def _gelu_tanh(x):
    return 0.5 * x * (1.0 + jnp.tanh(0.7978845608028654 * (x + 0.044715 * (x * x * x))))


def _layer_norm(z, g, b, eps):
    mu = jnp.mean(z, axis=-1, keepdims=True)
    zc = z - mu
    var = jnp.mean(zc * zc, axis=-1, keepdims=True)
    return zc * lax.rsqrt(var + eps) * g + b


def _const_spec(shape):
    nd = len(shape)
    return pl.BlockSpec(shape, lambda *_: (0,) * nd)


def _swiglu_ln(x, xb_ref, acc_ref, w1_ref, w3_ref, w2_ref, g, b):
    xb_ref[...] = x.astype(BF16)
    acc_ref[...] = jnp.zeros_like(acc_ref)

    def body(j, carry):
        xb = xb_ref[...]
        a = _dot(xb, w1_ref[j])
        c = _dot(xb, w3_ref[j])
        h = (a * _sigmoid(a)) * c
        acc_ref[...] += _dot(h.astype(BF16), w2_ref[j])
        return carry

    lax.fori_loop(0, N_FF_CHUNKS, body, 0)
    return _layer_norm(ALPHA * x + 0.5 * acc_ref[...], g, b, LN_EPS)


def _ffn_kernel(x_ref, w1_ref, w3_ref, w2_ref, g_ref, b_ref, o_ref, ob_ref, xb_ref, acc_ref):
    out = _swiglu_ln(x_ref[...], xb_ref, acc_ref, w1_ref, w3_ref, w2_ref, g_ref[...], b_ref[...])
    o_ref[...] = out
    ob_ref[...] = out.astype(BF16)


def _merge_ffn_kernel(x_ref, o_ref, part_ref, g0_ref, p0_ref, wout_ref, g1_ref, b1_ref,
                      w1_ref, w3_ref, w2_ref, g2_ref, b2_ref, out_ref, xb_ref, acc_ref):
    m0 = _dot(o_ref[...].astype(BF16), p0_ref[...])
    merged = part_ref[...].astype(F32) + g0_ref[...].astype(F32) * m0
    y = _dot(merged.astype(BF16), wout_ref[...])
    x2 = _layer_norm(ALPHA * x_ref[...] + y, g1_ref[...], b1_ref[...], LN_EPS)
    out_ref[...] = _swiglu_ln(x2, xb_ref, acc_ref, w1_ref, w3_ref, w2_ref, g2_ref[...], b2_ref[...])


def _token_specs(n_s, ts, width, batch_major):
    if batch_major:
        return pl.BlockSpec((ts, width), lambda i, b: (b * n_s + i, 0))
    return pl.BlockSpec((ts, width), lambda i, b: (i, b))


def _ffn_weight_specs():
    return [
        _const_spec((N_FF_CHUNKS, D_MODEL, FF_CHUNK)),
        _const_spec((N_FF_CHUNKS, D_MODEL, FF_CHUNK)),
        _const_spec((N_FF_CHUNKS, FF_CHUNK, D_MODEL)),
    ]


def _ffn_call(x, seq, ffn_w, ln_g, ln_b, in_batch_major):
    ts = min(TOKEN_TILE, seq)
    n_s = seq // ts
    out_shape = (seq, N_BATCH * D_MODEL)
    return pl.pallas_call(
        _ffn_kernel,
        grid=(n_s, N_BATCH),
        in_specs=[_token_specs(n_s, ts, D_MODEL, in_batch_major)] + _ffn_weight_specs()
        + [_const_spec((1, D_MODEL)), _const_spec((1, D_MODEL))],
        out_specs=[_token_specs(n_s, ts, D_MODEL, False)] * 2,
        out_shape=[jax.ShapeDtypeStruct(out_shape, F32), jax.ShapeDtypeStruct(out_shape, BF16)],
        scratch_shapes=[pltpu.VMEM((ts, D_MODEL), BF16), pltpu.VMEM((ts, D_MODEL), F32)],
        compiler_params=pltpu.CompilerParams(
            dimension_semantics=("arbitrary", "arbitrary"), vmem_limit_bytes=VMEM_LIMIT),
        name="ffn",
    )(x, *ffn_w, ln_g, ln_b)


def _merge_ffn_call(x, o_rwkv, part, g0, seq, p0, w_out, ln_g1, ln_b1, ffn_w, ln_g2, ln_b2,
                    out_batch_major):
    ts = min(TOKEN_TILE, seq)
    n_s = seq // ts
    out_shape = (N_BATCH * seq, D_MODEL) if out_batch_major else (seq, N_BATCH * D_MODEL)
    vec = _const_spec((1, D_MODEL))
    return pl.pallas_call(
        _merge_ffn_kernel,
        grid=(n_s, N_BATCH),
        in_specs=[
            _token_specs(n_s, ts, D_MODEL, False),
            _token_specs(n_s, ts, BRANCH_W, False),
            _token_specs(n_s, ts, D_MODEL, False),
            _token_specs(n_s, ts, D_MODEL, False),
            _const_spec((BRANCH_W, D_MODEL)),
            _const_spec((D_MODEL, D_MODEL)),
            vec, vec,
        ] + _ffn_weight_specs() + [vec, vec],
        out_specs=_token_specs(n_s, ts, D_MODEL, out_batch_major),
        out_shape=jax.ShapeDtypeStruct(out_shape, F32),
        scratch_shapes=[pltpu.VMEM((ts, D_MODEL), BF16), pltpu.VMEM((ts, D_MODEL), F32)],
        compiler_params=pltpu.CompilerParams(
            dimension_semantics=("arbitrary", "arbitrary"), vmem_limit_bytes=VMEM_LIMIT),
        name="merge_ffn",
    )(x, o_rwkv, part, g0, p0, w_out, ln_g1, ln_b1, *ffn_w, ln_g2, ln_b2)


RWKV_IN_TILE = WKV_CHUNK * N_BATCH


def _rwkv_in_kernel(*refs, has_vres):
    if has_vres:
        (x_ref, win_ref, mu_ref, w0_ref, w2_ref, a0_ref, a2_ref, g2_ref, kk_ref, ka_ref, rk_ref,
         gng_ref, gnb_ref, ones_ref, vf_ref, v0_ref, v1_ref, v2_ref,
         r_out, k_out, v_out, ka_out, bb_out, cum_out, gg_out, c2_out, zbuf_ref) = refs
    else:
        (x_ref, win_ref, mu_ref, w0_ref, w2_ref, a0_ref, a2_ref, g2_ref, kk_ref, ka_ref, rk_ref,
         gng_ref, gnb_ref, ones_ref,
         r_out, k_out, v_out, ka_out, bb_out, cum_out, gg_out, c2_out, vraw_out, zbuf_ref) = refs
    tm = RWKV_IN_TILE

    @pl.when(pl.program_id(0) == 0)
    def _():
        zbuf_ref[0:N_BATCH, :] = jnp.zeros((N_BATCH, RWKV_COLS), F32)

    zbuf_ref[N_BATCH:tm + N_BATCH, :] = _dot(x_ref[...], win_ref[...])
    z = zbuf_ref[N_BATCH:tm + N_BATCH, :]
    zp = zbuf_ref[0:tm, :]
    zm = z + mu_ref[...] * (zp - z)
    zbuf_ref[0:N_BATCH, :] = zbuf_ref[tm:tm + N_BATCH, :]

    r = zm[:, 0:512]
    k = zm[:, 512:1024]
    v = zm[:, 1024:1536]
    zwa = zm[:, 1536:1664]
    zg = zm[:, 1664:1792]

    if has_vres:
        lora = _dot(_dot(v.astype(BF16), v1_ref[...]).astype(BF16), v2_ref[...])
        v = v + (vf_ref[...] - v) * _sigmoid(v0_ref[...] + lora)
    else:
        vraw_out[...] = v

    w_log = -_softplus(-(w0_ref[...] + _dot(jnp.tanh(zwa).astype(BF16), w2_ref[...]))) - 0.5
    lw = -jnp.exp(w_log)
    a = _sigmoid(a0_ref[...] + _dot(zwa.astype(BF16), a2_ref[...]))
    g = _dot(_sigmoid(zg).astype(BF16), g2_ref[...])

    ones = ones_ref[...]
    kk = k * kk_ref[...]
    ss = _dot((kk * kk).astype(BF16), ones)
    kk = kk / jnp.maximum(jnp.sqrt(ss), 1e-12)
    k2 = k * (1.0 + (a - 1.0) * ka_ref[...])
    bonus = _dot((r * k2 * rk_ref[...]).astype(BF16), ones) * v

    r_out[...] = r
    k_out[...] = k2
    v_out[...] = v
    ka_out[...] = kk * jnp.exp(-lw)
    bb_out[...] = kk * a
    gg_out[...] = gng_ref[...] * g
    c2_out[...] = (gnb_ref[...] + bonus) * g

    acc = jnp.zeros((N_BATCH, BRANCH_W), F32)
    for t in range(WKV_CHUNK):
        acc = acc + lw[t * N_BATCH:(t + 1) * N_BATCH, :]
        cum_out[t * N_BATCH:(t + 1) * N_BATCH, :] = acc


def _rwkv_in_call(x, seq, params, v_first):
    tm = RWKV_IN_TILE
    n_rows = seq * N_BATCH
    has_vres = v_first is not None
    row = lambda w: pl.BlockSpec((tm, w), lambda i: (i, 0))
    vec = _const_spec((1, BRANCH_W))
    in_specs = [row(D_MODEL), _const_spec((D_MODEL, RWKV_COLS)), _const_spec((1, RWKV_COLS)),
                vec, _const_spec((128, BRANCH_W)), vec, _const_spec((128, BRANCH_W)),
                _const_spec((128, BRANCH_W)), vec, vec, vec, vec, vec,
                _const_spec((BRANCH_W, BRANCH_W))]
    args = [x] + list(params)
    n_out = 8
    if has_vres:
        in_specs += [row(BRANCH_W), vec, _const_spec((BRANCH_W, 128)), _const_spec((128, BRANCH_W))]
        args += list(v_first)
    else:
        n_out = 9
    return pl.pallas_call(
        functools.partial(_rwkv_in_kernel, has_vres=has_vres),
        grid=(n_rows // tm,),
        in_specs=in_specs,
        out_specs=[row(BRANCH_W)] * n_out,
        out_shape=[jax.ShapeDtypeStruct((n_rows, BRANCH_W), F32)] * n_out,
        scratch_shapes=[pltpu.VMEM((tm + N_BATCH, RWKV_COLS), F32)],
        compiler_params=pltpu.CompilerParams(
            dimension_semantics=("arbitrary",), vmem_limit_bytes=VMEM_LIMIT),
        name="rwkv_in_vres" if has_vres else "rwkv_in",
    )(*args)


MIX_TILE = GMLP_CHUNK * N_BATCH
CONV_WIDTH = 4
CONV_HALO = (CONV_WIDTH - 1) * N_BATCH


def _mix_in_kernel(x_ref, wgu_ref, wgv_ref, wlx_ref, wly_ref, wgate_ref, gateb_ref,
                   lng_ref, lnb_ref, wk_ref, sb_ref, convw_ref, convb_ref, wa_ref, ba_ref,
                   wx_ref, bx_ref, lam_ref, p1_ref, p2_ref,
                   part_out, g0_out, xbuf_ref, a_ref, b_ref, hcar_ref):
    tm = MIX_TILE

    @pl.when(pl.program_id(0) == 0)
    def _():
        xbuf_ref[0:CONV_HALO, :] = jnp.zeros((CONV_HALO, BRANCH_W), F32)
        hcar_ref[...] = jnp.zeros_like(hcar_ref)

    xb = x_ref[...]

    u = _gelu_tanh(_dot(xb, wgu_ref[...]))
    vv = _layer_norm(_gelu_tanh(_dot(xb, wgv_ref[...])), lng_ref[...], lnb_ref[...], LN_EPS)
    vvb = vv.astype(BF16)
    s = jnp.concatenate(
        [_dot(wk_ref[g], vvb[:, g * GMLP_GROUP_DIM:(g + 1) * GMLP_GROUP_DIM])
         for g in range(GMLP_GROUPS)], axis=1)
    o_gmlp = u * (s + sb_ref[...])

    xbuf_ref[CONV_HALO:tm + CONV_HALO, :] = _dot(xb, wlx_ref[...])
    xc = convb_ref[...]
    for j in range(CONV_WIDTH):
        xc = xc + xbuf_ref[j * N_BATCH:j * N_BATCH + tm, :] * convw_ref[j:j + 1, :]
    xbuf_ref[0:CONV_HALO, :] = xbuf_ref[tm:tm + CONV_HALO, :]
    xcb = xc.astype(BF16)
    rg = _sigmoid(_dot(xcb, wa_ref[...]) + ba_ref[...])
    ig = _sigmoid(_dot(xcb, wx_ref[...]) + bx_ref[...])
    log_a = (-LRU_C) * rg * _softplus(-lam_ref[...])
    th = jnp.tanh(log_a)
    a_ref[...] = jnp.exp(log_a)
    b_ref[...] = jnp.sqrt(-2.0 * th / (1.0 - th)) * (ig * xc)

    def step(t, h):
        rows = pl.ds(pl.multiple_of(t * N_BATCH, N_BATCH), N_BATCH)
        h = a_ref[rows, :] * h + b_ref[rows, :]
        b_ref[rows, :] = h
        return h

    hcar_ref[...] = lax.fori_loop(0, tm // N_BATCH, step, hcar_ref[...], unroll=8)
    o_lru = _gelu_tanh(_dot(xb, wly_ref[...])) * b_ref[...]

    g1 = _sigmoid(_dot(xb, wgate_ref[1]) + gateb_ref[1:2, :])
    part = g1 * _dot(o_gmlp.astype(BF16), p1_ref[...])
    g2 = _sigmoid(_dot(xb, wgate_ref[2]) + gateb_ref[2:3, :])
    part_out[...] = (part + g2 * _dot(o_lru.astype(BF16), p2_ref[...])).astype(BF16)
    g0_out[...] = _sigmoid(_dot(xb, wgate_ref[0]) + gateb_ref[0:1, :]).astype(BF16)


def _mix_in_call(x, seq, params):
    tm = MIX_TILE
    n_rows = seq * N_BATCH
    row = lambda w: pl.BlockSpec((tm, w), lambda i: (i, 0))
    vec = _const_spec((1, BRANCH_W))
    wcol = _const_spec((D_MODEL, BRANCH_W))
    sq = _const_spec((BRANCH_W, BRANCH_W))
    in_specs = [row(D_MODEL), wcol, wcol, wcol, wcol, _const_spec((3, D_MODEL, D_MODEL)),
                _const_spec((3, D_MODEL)), vec, vec, _const_spec((GMLP_GROUPS, tm, tm)),
                _const_spec((tm, BRANCH_W)), _const_spec((CONV_WIDTH, BRANCH_W)), vec,
                sq, vec, sq, vec, vec,
                _const_spec((BRANCH_W, D_MODEL)), _const_spec((BRANCH_W, D_MODEL))]
    return pl.pallas_call(
        _mix_in_kernel,
        grid=(n_rows // tm,),
        in_specs=in_specs,
        out_specs=[row(D_MODEL), row(D_MODEL)],
        out_shape=[jax.ShapeDtypeStruct((n_rows, D_MODEL), BF16)] * 2,
        scratch_shapes=[pltpu.VMEM((tm + CONV_HALO, BRANCH_W), F32),
                        pltpu.VMEM((tm, BRANCH_W), F32), pltpu.VMEM((tm, BRANCH_W), F32),
                        pltpu.VMEM((N_BATCH, BRANCH_W), F32)],
        compiler_params=pltpu.CompilerParams(
            dimension_semantics=("arbitrary",), vmem_limit_bytes=VMEM_LIMIT),
        name="mix_in",
    )(x, *params)


def _block_diag(y, mask):
    return jnp.concatenate([y] * HEADS_PER_GROUP, axis=0) * mask


def _wkv_kernel(r_ref, k_ref, v_ref, ka_ref, bb_ref, cum_ref, gg_ref, c2_ref,
                mstrict_ref, mincl_ref, eye_ref, mbd_ref, mbdf_ref, ones_ref,
                o_ref, state_ref):
    C = WKV_CHUNK

    @pl.when(pl.program_id(1) == 0)
    def _():
        state_ref[...] = jnp.zeros_like(state_ref)

    cum = cum_ref[...]
    ref = cum[C // 2 - 1:C // 2, :]
    e_fwd = jnp.exp(cum - ref)
    e_bwd = jnp.exp(ref - cum)
    e_ref = jnp.exp(ref)
    e_end = jnp.exp(cum[C - 1:C, :] - ref)
    x_all = jnp.concatenate([-ka_ref[...] * e_fwd, r_ref[...] * e_fwd], axis=0).astype(BF16)
    kt = (k_ref[...] * e_bwd).astype(BF16)
    bt = (bb_ref[...] * e_bwd).astype(BF16)
    vb = v_ref[...].astype(BF16)
    mbd = mbd_ref[...]
    m_strict = mstrict_ref[...]
    m_incl = mincl_ref[...]

    outs = []
    for g in range(N_GROUPS):
        sl = slice(g * GROUP_W, (g + 1) * GROUP_W)
        xg = x_all[:, sl]
        ktg, btg, vg = kt[:, sl], bt[:, sl], vb[:, sl]
        y_exp = jnp.concatenate(
            [_block_diag(btg, mbd), _block_diag(ktg, mbd)], axis=0)
        sc = _dot_nt(xg, y_exp)
        l_mat = sc[0:C, 0:GROUP_W] * m_strict
        a_ak = (sc[0:C, GROUP_W:] * m_strict).astype(BF16)
        a_qb = (sc[C:, 0:GROUP_W] * m_incl).astype(BF16)
        a_qk = (sc[C:, GROUP_W:] * m_incl).astype(BF16)

        s_ref = state_ref[g] * e_ref[:, sl]
        p = _dot_nt(xg, s_ref.astype(BF16))
        v_bd = _block_diag(vg, mbd)
        rhs = p[0:C] + _dot(a_ak, v_bd)

        t_mat = eye_ref[...] + l_mat
        l_pow = l_mat
        for _ in range(5):
            l_pow = _dot(l_pow.astype(BF16), _block_diag(l_pow.astype(BF16), mbd))
            t_mat = t_mat + _dot(t_mat.astype(BF16), _block_diag(l_pow.astype(BF16), mbd))
        u = _dot(t_mat.astype(BF16), _block_diag(rhs.astype(BF16), mbd))
        ub = u.astype(BF16)
        outs.append(p[C:] + _dot(a_qk, v_bd) + _dot(a_qb, _block_diag(ub, mbd)))

        vu = jnp.concatenate([vg, ub], axis=0)
        kb = jnp.concatenate([ktg, btg], axis=0)
        state_ref[g] = (s_ref + _dot_tn(vu, kb)) * mbdf_ref[...] * e_end[:, sl]

    o = jnp.concatenate(outs, axis=1)
    ones = ones_ref[...]
    mu = _dot(o.astype(BF16), ones) * (1.0 / HEAD_DIM)
    oc = o - mu
    var = _dot((oc * oc).astype(BF16), ones) * (1.0 / HEAD_DIM)
    o_ref[...] = oc * lax.rsqrt(var + GN_EPS) * gg_ref[...] + c2_ref[...]


def _wkv_consts():
    C = WKV_CHUNK
    t = jnp.arange(C)[:, None]
    col = jnp.arange(GROUP_W)[None, :]
    s = col % C
    m_strict = (t > s).astype(F32)
    m_incl = (t >= s).astype(F32)
    eye = (t == s).astype(F32)
    rows = jnp.arange(GROUP_W)[:, None]
    mbd = (rows // C == col // HEAD_DIM)
    hh = jnp.arange(BRANCH_W) // HEAD_DIM
    ones = (hh[:, None] == hh[None, :]).astype(BF16)
    return m_strict, m_incl, eye, mbd.astype(BF16), mbd.astype(F32), ones


def _wkv_call(seq, r, k, v, ka, bb, cum, gg, c2):
    C = WKV_CHUNK
    blk = pl.BlockSpec((C, BRANCH_W), lambda b, i: (i, b))
    consts = _wkv_consts()
    view = lambda a: a.reshape(seq, N_BATCH * BRANCH_W)
    out = pl.pallas_call(
        _wkv_kernel,
        grid=(N_BATCH, seq // C),
        in_specs=[blk] * 8 + [_const_spec(c.shape) for c in consts],
        out_specs=blk,
        out_shape=jax.ShapeDtypeStruct((seq, N_BATCH * BRANCH_W), F32),
        scratch_shapes=[pltpu.VMEM((N_GROUPS, GROUP_W, GROUP_W), F32)],
        compiler_params=pltpu.CompilerParams(
            dimension_semantics=("arbitrary", "arbitrary"), vmem_limit_bytes=VMEM_LIMIT),
        name="wkv",
    )(*[view(a) for a in (r, k, v, ka, bb, cum, gg, c2)], *consts)
    return out


def _ffn_weights(w1, w3, w2):
    up = lambda w: w.astype(BF16).reshape(D_MODEL, N_FF_CHUNKS, FF_CHUNK).transpose(1, 0, 2)
    return up(w1), up(w3), w2.astype(BF16).reshape(N_FF_CHUNKS, FF_CHUNK, D_MODEL)


def _row(p):
    return p.reshape(1, -1).astype(F32)


def _pad_rows(w, top, total):
    out = jnp.zeros((total, w.shape[1]), w.dtype)
    return lax.dynamic_update_slice(out, w, (top, 0))


def _block_diag_heads(w):
    h, n, m = w.shape
    eye = jnp.eye(h, dtype=w.dtype)
    return (eye[:, None, :, None] * w[:, :, None, :]).reshape(h * n, h * m)


def kernel(x, ln_g, ln_b, ffn_w1, ffn_w3, ffn_w2, w_in, gate_b, p_branch, w_out, rwkv_mu, rwkv_w0, rwkv_w2, rwkv_a0, rwkv_a2, rwkv_g2, rwkv_k_k, rwkv_k_a, rwkv_r_k, rwkv_gn_g, rwkv_gn_b, rwkv_v0, rwkv_v1, rwkv_v2, gmlp_ln_g, gmlp_ln_b, gmlp_ws, gmlp_sb, lru_conv_w, lru_conv_b, lru_wa, lru_ba, lru_wx, lru_bx, lru_lam):
    n_batch, seq, d_model = x.shape
    assert n_batch == N_BATCH and d_model == D_MODEL
    assert seq % GMLP_CHUNK == 0 and seq % min(TOKEN_TILE, seq) == 0
    depth = ln_g.shape[0]
    hh = jnp.arange(BRANCH_W) // HEAD_DIM
    head_ones = (hh[:, None] == hh[None, :]).astype(BF16)
    causal = jnp.tril(jnp.ones((GMLP_CHUNK, GMLP_CHUNK), F32))
    eye_b = jnp.eye(N_BATCH, dtype=F32)

    cur = x.reshape(n_batch * seq, d_model)
    batch_major = True
    v_first = None
    for l in range(depth):
        cur, cur_b = _ffn_call(cur, seq, _ffn_weights(ffn_w1[l, 0], ffn_w3[l, 0], ffn_w2[l, 0]),
                               _row(ln_g[l, 0]), _row(ln_b[l, 0]), batch_major)
        batch_major = False
        rows = cur_b.reshape(seq * n_batch, d_model)

        win = w_in[l].astype(BF16)
        rwkv_params = (
            win[:, :RWKV_COLS], _row(rwkv_mu[l]), _row(rwkv_w0[l]),
            _pad_rows(rwkv_w2[l].astype(BF16), 0, 128), _row(rwkv_a0[l]),
            _pad_rows(rwkv_a2[l].astype(BF16), 64, 128), rwkv_g2[l].astype(BF16),
            _row(rwkv_k_k[l]), _row(rwkv_k_a[l]), _row(rwkv_r_k[l]),
            _row(rwkv_gn_g[l]), _row(rwkv_gn_b[l]), head_ones)
        if l == 0:
            vres = None
        else:
            v1p = jnp.zeros((BRANCH_W, 128), BF16).at[:, :rwkv_v1.shape[2]].set(
                rwkv_v1[l - 1].astype(BF16))
            vres = (v_first, _row(rwkv_v0[l - 1]), v1p,
                    _pad_rows(rwkv_v2[l - 1].astype(BF16), 0, 128))
        outs = _rwkv_in_call(rows, seq, rwkv_params, vres)
        if l == 0:
            v_first = outs[8]
        o_rwkv = _wkv_call(seq, *outs[:8])

        wk = jnp.stack([jnp.kron(gmlp_ws[l, g] * causal, eye_b) for g in range(GMLP_GROUPS)])
        sb_full = jnp.repeat(jnp.repeat(gmlp_sb[l].T, N_BATCH, axis=0), GMLP_GROUP_DIM, axis=1)
        mix_params = (
            win[:, 1792:2304], win[:, 2304:2816], win[:, 2816:3328], win[:, 3328:3840],
            win[:, 3840:].reshape(D_MODEL, 3, D_MODEL).transpose(1, 0, 2), gate_b[l].astype(F32),
            _row(gmlp_ln_g[l]), _row(gmlp_ln_b[l]), wk.astype(BF16), sb_full.astype(F32),
            lru_conv_w[l].astype(F32), _row(lru_conv_b[l]),
            _block_diag_heads(lru_wa[l]).astype(BF16), _row(lru_ba[l]),
            _block_diag_heads(lru_wx[l]).astype(BF16), _row(lru_bx[l]), _row(lru_lam[l]),
            p_branch[l, 1].astype(BF16), p_branch[l, 2].astype(BF16))
        part, g0 = _mix_in_call(rows, seq, mix_params)

        last = l == depth - 1
        cur = _merge_ffn_call(
            cur, o_rwkv, part.reshape(seq, n_batch * d_model), g0.reshape(seq, n_batch * d_model),
            seq, p_branch[l, 0].astype(BF16), w_out[l].astype(BF16),
            _row(ln_g[l, 1]), _row(ln_b[l, 1]),
            _ffn_weights(ffn_w1[l, 1], ffn_w3[l, 1], ffn_w2[l, 1]),
            _row(ln_g[l, 2]), _row(ln_b[l, 2]), last)
        batch_major = last
    return cur.reshape(n_batch, seq, d_model)
```

```python
import functools

import jax
import jax.numpy as jnp
from jax import lax
from jax.experimental import pallas as pl
from jax.experimental.pallas import tpu as pltpu

F32 = jnp.float32
BF16 = jnp.bfloat16

D_MODEL = 1024
D_FF = 2816
FF_CHUNK = 256
N_FF_CHUNKS = D_FF // FF_CHUNK
BRANCH_W = 512
HEAD_DIM = 64
N_HEADS = BRANCH_W // HEAD_DIM
RWKV_COLS = 1792
GMLP_GROUPS = 4
GMLP_GROUP_DIM = 128
GMLP_CHUNK = 128
WKV_CHUNK = 64
HEADS_PER_GROUP = 4
GROUP_W = HEADS_PER_GROUP * HEAD_DIM
N_GROUPS = N_HEADS // HEADS_PER_GROUP
CONV_WIDTH = 4
SUBLANES = 8
LN_EPS = 1e-5
GN_EPS = 64e-5
LRU_C = 8.0
DEPTH = 2
ALPHA = (2 * DEPTH) ** 0.25
FFN_TILE = 1024
MERGE_TILE = 512
SEQ_TILE = 512
VMEM_LIMIT = 56 * 1024 * 1024


def _dot(a, b):
    return jnp.dot(a, b, preferred_element_type=F32)


def _dot_nt(a, b):
    return lax.dot_general(a, b, (((1,), (1,)), ((), ())), preferred_element_type=F32)


def _dot_tn(a, b):
    return lax.dot_general(a, b, (((0,), (0,)), ((), ())), preferred_element_type=F32)


def _sigmoid(x):
    return 1.0 / (1.0 + jnp.exp(-x))


def _softplus(x):
    return jnp.maximum(x, 0.0) + jnp.log(1.0 + jnp.exp(-jnp.abs(x)))


def _gelu_tanh(x):
    return 0.5 * x * (1.0 + jnp.tanh(0.7978845608028654 * (x + 0.044715 * (x * x * x))))


def _layer_norm(z, g, b, eps):
    mu = jnp.mean(z, axis=-1, keepdims=True)
    zc = z - mu
    var = jnp.mean(zc * zc, axis=-1, keepdims=True)
    return zc * lax.rsqrt(var + eps) * g + b


def _const_spec(shape):
    nd = len(shape)
    return pl.BlockSpec(shape, lambda *_: (0,) * nd)


def _params(n_grid_axes):
    return pltpu.CompilerParams(
        dimension_semantics=("arbitrary",) * n_grid_axes, vmem_limit_bytes=VMEM_LIMIT)


def _swiglu_ln(x, xb_ref, acc_ref, w1_ref, w3_ref, w2_ref, g, b):
    xb_ref[...] = x.astype(BF16)
    acc_ref[...] = jnp.zeros_like(acc_ref)

    def body(j, carry):
        xb = xb_ref[...]
        a = _dot(xb, w1_ref[j])
        c = _dot(xb, w3_ref[j])
        h = (a * _sigmoid(a)) * c
        acc_ref[...] += _dot(h.astype(BF16), w2_ref[j])
        return carry

    lax.fori_loop(0, N_FF_CHUNKS, body, 0)
    return _layer_norm(ALPHA * x + 0.5 * acc_ref[...], g, b, LN_EPS)


def _ffn_kernel(x_ref, w1_ref, w3_ref, w2_ref, g_ref, b_ref, o_ref, ob_ref, xb_ref, acc_ref):
    out = _swiglu_ln(x_ref[...], xb_ref, acc_ref, w1_ref, w3_ref, w2_ref, g_ref[...], b_ref[...])
    o_ref[...] = out
    ob_ref[...] = out.astype(BF16)


def _merge_ffn_kernel(x_ref, o_ref, part_ref, g0_ref, p0_ref, wout_ref, g1_ref, b1_ref,
                      w1_ref, w3_ref, w2_ref, g2_ref, b2_ref, out_ref, xb_ref, acc_ref):
    m0 = _dot(o_ref[...].astype(BF16), p0_ref[...])
    merged = part_ref[...].astype(F32) + g0_ref[...].astype(F32) * m0
    y = _dot(merged.astype(BF16), wout_ref[...])
    x2 = _layer_norm(ALPHA * x_ref[...] + y, g1_ref[...], b1_ref[...], LN_EPS)
    out_ref[...] = _swiglu_ln(x2, xb_ref, acc_ref, w1_ref, w3_ref, w2_ref, g2_ref[...], b2_ref[...])


def _rows_spec(tm, width):
    return pl.BlockSpec((tm, width), lambda i: (i, 0))


def _ffn_weight_specs():
    return [
        _const_spec((N_FF_CHUNKS, D_MODEL, FF_CHUNK)),
        _const_spec((N_FF_CHUNKS, D_MODEL, FF_CHUNK)),
        _const_spec((N_FF_CHUNKS, FF_CHUNK, D_MODEL)),
    ]


def _ffn_call(x, ffn_w, ln_g, ln_b):
    n_rows = x.shape[0]
    tm = min(FFN_TILE, n_rows)
    vec = _const_spec((1, D_MODEL))
    return pl.pallas_call(
        _ffn_kernel,
        grid=(n_rows // tm,),
        in_specs=[_rows_spec(tm, D_MODEL)] + _ffn_weight_specs() + [vec, vec],
        out_specs=[_rows_spec(tm, D_MODEL)] * 2,
        out_shape=[jax.ShapeDtypeStruct((n_rows, D_MODEL), F32),
                   jax.ShapeDtypeStruct((n_rows, D_MODEL), BF16)],
        scratch_shapes=[pltpu.VMEM((tm, D_MODEL), BF16), pltpu.VMEM((tm, D_MODEL), F32)],
        compiler_params=_params(1),
        name="ffn",
    )(x, *ffn_w, ln_g, ln_b)


def _merge_ffn_call(x, o_rwkv, part, g0, p0, w_out, ln_g1, ln_b1, ffn_w, ln_g2, ln_b2):
    n_rows = x.shape[0]
    tm = min(MERGE_TILE, n_rows)
    vec = _const_spec((1, D_MODEL))
    return pl.pallas_call(
        _merge_ffn_kernel,
        grid=(n_rows // tm,),
        in_specs=[
            _rows_spec(tm, D_MODEL), _rows_spec(tm, BRANCH_W), _rows_spec(tm, D_MODEL),
            _rows_spec(tm, D_MODEL), _const_spec((BRANCH_W, D_MODEL)),
            _const_spec((D_MODEL, D_MODEL)), vec, vec,
        ] + _ffn_weight_specs() + [vec, vec],
        out_specs=_rows_spec(tm, D_MODEL),
        out_shape=jax.ShapeDtypeStruct((n_rows, D_MODEL), F32),
        scratch_shapes=[pltpu.VMEM((tm, D_MODEL), BF16), pltpu.VMEM((tm, D_MODEL), F32)],
        compiler_params=_params(1),
        name="merge_ffn",
    )(x, o_rwkv, part, g0, p0, w_out, ln_g1, ln_b1, *ffn_w, ln_g2, ln_b2)


def _seq_spec(n_tiles, tm, width):
    return pl.BlockSpec((tm, width), lambda b, i: (b * n_tiles + i, 0))


def _rwkv_in_kernel(*refs, has_vres):
    if has_vres:
        (x_ref, win_ref, mu_ref, w0_ref, w2_ref, a0_ref, a2_ref, g2_ref, kk_ref, ka_ref, rk_ref,
         gng_ref, gnb_ref, ones_ref, tri_ref, vf_ref, v0_ref, v1_ref, v2_ref,
         r_out, k_out, v_out, ka_out, bb_out, cum_out, gg_out, c2_out, zbuf_ref) = refs
    else:
        (x_ref, win_ref, mu_ref, w0_ref, w2_ref, a0_ref, a2_ref, g2_ref, kk_ref, ka_ref, rk_ref,
         gng_ref, gnb_ref, ones_ref, tri_ref,
         r_out, k_out, v_out, ka_out, bb_out, cum_out, gg_out, c2_out, vraw_out, zbuf_ref) = refs
    tm = x_ref.shape[0]
    base = SUBLANES

    @pl.when(pl.program_id(1) == 0)
    def _():
        zbuf_ref[0:base, :] = jnp.zeros((base, RWKV_COLS), F32)

    zbuf_ref[base:base + tm, :] = _dot(x_ref[...], win_ref[...])
    z = zbuf_ref[base:base + tm, :]
    zp = zbuf_ref[base - 1:base - 1 + tm, :]
    zm = z + mu_ref[...] * (zp - z)
    zbuf_ref[base - 1:base, :] = zbuf_ref[base + tm - 1:base + tm, :]

    r = zm[:, 0:512]
    k = zm[:, 512:1024]
    v = zm[:, 1024:1536]
    zwa = zm[:, 1536:1664]
    zg = zm[:, 1664:1792]

    if has_vres:
        lora = _dot(_dot(v.astype(BF16), v1_ref[...]).astype(BF16), v2_ref[...])
        v = v + (vf_ref[...] - v) * _sigmoid(v0_ref[...] + lora)
    else:
        vraw_out[...] = v

    w_log = -_softplus(-(w0_ref[...] + _dot(jnp.tanh(zwa).astype(BF16), w2_ref[...]))) - 0.5
    lw = -jnp.exp(w_log)
    a = _sigmoid(a0_ref[...] + _dot(zwa.astype(BF16), a2_ref[...]))
    g = _dot(_sigmoid(zg).astype(BF16), g2_ref[...])

    ones = ones_ref[...]
    kk = k * kk_ref[...]
    ss = _dot((kk * kk).astype(BF16), ones)
    kk = kk / jnp.maximum(jnp.sqrt(ss), 1e-12)
    k2 = k * (1.0 + (a - 1.0) * ka_ref[...])
    bonus = _dot((r * k2 * rk_ref[...]).astype(BF16), ones) * v

    r_out[...] = r
    k_out[...] = k2
    v_out[...] = v
    ka_out[...] = kk * jnp.exp(-lw)
    bb_out[...] = kk * a
    gg_out[...] = gng_ref[...] * g
    c2_out[...] = (gnb_ref[...] + bonus) * g

    tri = tri_ref[...]
    hi = lw.astype(BF16)
    rem = lw - hi.astype(F32)
    mid = rem.astype(BF16)
    lo = (rem - mid.astype(F32)).astype(BF16)
    cum_out[...] = _dot(tri, hi) + _dot(tri, mid) + _dot(tri, lo)


def _rwkv_in_call(x, n_batch, seq, params, v_first):
    tm = min(SEQ_TILE, seq)
    n_tiles = seq // tm
    n_rows = n_batch * seq
    has_vres = v_first is not None
    row = lambda w: _seq_spec(n_tiles, tm, w)
    vec = _const_spec((1, BRANCH_W))
    in_specs = [row(D_MODEL), _const_spec((D_MODEL, RWKV_COLS)), _const_spec((1, RWKV_COLS)),
                vec, _const_spec((128, BRANCH_W)), vec, _const_spec((128, BRANCH_W)),
                _const_spec((128, BRANCH_W)), vec, vec, vec, vec, vec,
                _const_spec((BRANCH_W, BRANCH_W)), _const_spec((tm, tm))]
    r_idx = jnp.arange(tm)
    tri = ((r_idx[:, None] >= r_idx[None, :])
           & (r_idx[:, None] // WKV_CHUNK == r_idx[None, :] // WKV_CHUNK)).astype(BF16)
    args = [x] + list(params) + [tri]
    n_out = 8
    if has_vres:
        in_specs += [row(BRANCH_W), vec, _const_spec((BRANCH_W, 128)), _const_spec((128, BRANCH_W))]
        args += list(v_first)
    else:
        n_out = 9
    return pl.pallas_call(
        functools.partial(_rwkv_in_kernel, has_vres=has_vres),
        grid=(n_batch, n_tiles),
        in_specs=in_specs,
        out_specs=[row(BRANCH_W)] * n_out,
        out_shape=[jax.ShapeDtypeStruct((n_rows, BRANCH_W), F32)] * n_out,
        scratch_shapes=[pltpu.VMEM((tm + SUBLANES, RWKV_COLS), F32)],
        compiler_params=_params(2),
        name="rwkv_in_vres" if has_vres else "rwkv_in",
    )(*args)


def _mix_in_kernel(x_ref, wgu_ref, wgv_ref, wlx_ref, wly_ref, wgate_ref, gateb_ref,
                   lng_ref, lnb_ref, ws_ref, sb_ref, convw_ref, convb_ref, wa_ref, ba_ref,
                   wx_ref, bx_ref, lam_ref, p1_ref, p2_ref,
                   part_out, g0_out, xbuf_ref, a_ref, b_ref, hcar_ref):
    tm = x_ref.shape[0]
    base = SUBLANES

    @pl.when(pl.program_id(1) == 0)
    def _():
        xbuf_ref[0:base, :] = jnp.zeros((base, BRANCH_W), F32)
        hcar_ref[...] = jnp.zeros_like(hcar_ref)

    xb = x_ref[...]

    u = _gelu_tanh(_dot(xb, wgu_ref[...]))
    vv = _layer_norm(_gelu_tanh(_dot(xb, wgv_ref[...])), lng_ref[...], lnb_ref[...], LN_EPS)
    vvb = vv.astype(BF16)
    s = jnp.concatenate([
        jnp.concatenate([
            _dot(ws_ref[g], vvb[c * GMLP_CHUNK:(c + 1) * GMLP_CHUNK,
                                g * GMLP_GROUP_DIM:(g + 1) * GMLP_GROUP_DIM])
            for g in range(GMLP_GROUPS)], axis=1)
        for c in range(tm // GMLP_CHUNK)], axis=0)
    o_gmlp = u * (s + sb_ref[...])

    xbuf_ref[base:base + tm, :] = _dot(xb, wlx_ref[...])
    xc = convb_ref[...]
    for j in range(CONV_WIDTH):
        off = base - (CONV_WIDTH - 1) + j
        xc = xc + xbuf_ref[off:off + tm, :] * convw_ref[j:j + 1, :]
    hist = CONV_WIDTH - 1
    xbuf_ref[base - hist:base, :] = xbuf_ref[base + tm - hist:base + tm, :]
    xcb = xc.astype(BF16)
    rg = _sigmoid(_dot(xcb, wa_ref[...]) + ba_ref[...])
    ig = _sigmoid(_dot(xcb, wx_ref[...]) + bx_ref[...])
    log_a = (-LRU_C) * rg * _softplus(-lam_ref[...])
    th = jnp.tanh(log_a)
    a = jnp.exp(log_a)
    bc = jnp.sqrt(-2.0 * th / (1.0 - th)) * (ig * xc)

    sub = lax.broadcasted_iota(jnp.int32, (tm, BRANCH_W), 0) % SUBLANES
    for d in (1, 2, 4):
        keep = sub >= d
        a_sh = pltpu.roll(a, d, axis=0)
        b_sh = pltpu.roll(bc, d, axis=0)
        bc = jnp.where(keep, bc + a * b_sh, bc)
        a = jnp.where(keep, a * a_sh, a)
    a_ref[...] = a
    b_ref[...] = bc

    def step(q, hc):
        rows = pl.ds(pl.multiple_of(q * SUBLANES, SUBLANES), SUBLANES)
        h = b_ref[rows, :] + a_ref[rows, :] * hc
        b_ref[rows, :] = h
        return jnp.broadcast_to(h[SUBLANES - 1:SUBLANES, :], (SUBLANES, BRANCH_W))

    hcar_ref[...] = lax.fori_loop(0, tm // SUBLANES, step, hcar_ref[...], unroll=8)
    o_lru = _gelu_tanh(_dot(xb, wly_ref[...])) * b_ref[...]

    g1 = _sigmoid(_dot(xb, wgate_ref[1]) + gateb_ref[1:2, :])
    part = g1 * _dot(o_gmlp.astype(BF16), p1_ref[...])
    g2 = _sigmoid(_dot(xb, wgate_ref[2]) + gateb_ref[2:3, :])
    part_out[...] = (part + g2 * _dot(o_lru.astype(BF16), p2_ref[...])).astype(BF16)
    g0_out[...] = _sigmoid(_dot(xb, wgate_ref[0]) + gateb_ref[0:1, :]).astype(BF16)


def _mix_in_call(x, n_batch, seq, params):
    tm = min(SEQ_TILE, seq)
    n_tiles = seq // tm
    n_rows = n_batch * seq
    row = lambda w: _seq_spec(n_tiles, tm, w)
    vec = _const_spec((1, BRANCH_W))
    wcol = _const_spec((D_MODEL, BRANCH_W))
    sq = _const_spec((BRANCH_W, BRANCH_W))
    in_specs = [row(D_MODEL), wcol, wcol, wcol, wcol, _const_spec((3, D_MODEL, D_MODEL)),
                _const_spec((3, D_MODEL)), vec, vec,
                _const_spec((GMLP_GROUPS, GMLP_CHUNK, GMLP_CHUNK)),
                _const_spec((tm, BRANCH_W)), _const_spec((CONV_WIDTH, BRANCH_W)), vec,
                sq, vec, sq, vec, vec,
                _const_spec((BRANCH_W, D_MODEL)), _const_spec((BRANCH_W, D_MODEL))]
    return pl.pallas_call(
        _mix_in_kernel,
        grid=(n_batch, n_tiles),
        in_specs=in_specs,
        out_specs=[row(D_MODEL), row(D_MODEL)],
        out_shape=[jax.ShapeDtypeStruct((n_rows, D_MODEL), BF16)] * 2,
        scratch_shapes=[pltpu.VMEM((tm + SUBLANES, BRANCH_W), F32),
                        pltpu.VMEM((tm, BRANCH_W), F32), pltpu.VMEM((tm, BRANCH_W), F32),
                        pltpu.VMEM((SUBLANES, BRANCH_W), F32)],
        compiler_params=_params(2),
        name="mix_in",
    )(x, *params)


def _block_diag(y, mask):
    return jnp.concatenate([y] * HEADS_PER_GROUP, axis=0) * mask


def _wkv_kernel(r_ref, k_ref, v_ref, ka_ref, bb_ref, cum_ref, gg_ref, c2_ref,
                mstrict_ref, mincl_ref, eye_ref, mbd_ref, mbdf_ref, ones_ref,
                o_ref, state_ref):
    n_batch = r_ref.shape[0]
    C = WKV_CHUNK

    @pl.when(pl.program_id(0) == 0)
    def _():
        state_ref[...] = jnp.zeros_like(state_ref)

    m_strict, m_incl, eye = mstrict_ref[...], mincl_ref[...], eye_ref[...]
    mbd, mbdf = mbd_ref[...], mbdf_ref[...]

    cum = cum_ref[...]
    ref = cum[:, C // 2 - 1:C // 2, :]
    e_fwd = jnp.exp(cum - ref)
    e_bwd = jnp.exp(ref - cum)
    e_ref = jnp.exp(ref)
    e_end = jnp.exp(cum[:, C - 1:C, :] - ref)
    a_hat = (-ka_ref[...] * e_fwd).astype(BF16)
    r_hat = (r_ref[...] * e_fwd).astype(BF16)
    kt = (k_ref[...] * e_bwd).astype(BF16)
    bt = (bb_ref[...] * e_bwd).astype(BF16)
    vb = v_ref[...].astype(BF16)

    chains = [(b, g) for b in range(n_batch) for g in range(N_GROUPS)]
    cols = lambda g: slice(g * GROUP_W, (g + 1) * GROUP_W)
    pick = lambda arr: [arr[b][:, cols(g)] for b, g in chains]
    ktg, btg, vg = pick(kt), pick(bt), pick(vb)
    xg = [jnp.concatenate([ah, rh], axis=0) for ah, rh in zip(pick(a_hat), pick(r_hat))]

    sc = [_dot_nt(x, jnp.concatenate([_block_diag(b_, mbd), _block_diag(k_, mbd)], axis=0))
          for x, b_, k_ in zip(xg, btg, ktg)]
    s_sc = [state_ref[b, g] * e_ref[b][:, cols(g)] for b, g in chains]
    p = [_dot_nt(x, s.astype(BF16)) for x, s in zip(xg, s_sc)]
    l_mat = [s[0:C, 0:GROUP_W] * m_strict for s in sc]
    a_k = [jnp.concatenate([s[0:C, GROUP_W:] * m_strict, s[C:, GROUP_W:] * m_incl],
                           axis=0).astype(BF16) for s in sc]
    a_qb = [(s[C:, 0:GROUP_W] * m_incl).astype(BF16) for s in sc]
    av = [_dot(a, _block_diag(v_, mbd)) for a, v_ in zip(a_k, vg)]

    lb = [l.astype(BF16) for l in l_mat]
    l_pow = [_dot(l, _block_diag(l, mbd)) for l in lb]
    t_mat = [eye + l for l in l_mat]
    for _ in range(4):
        lpb = [l.astype(BF16) for l in l_pow]
        res = [_dot(jnp.concatenate([t.astype(BF16), l], axis=0), _block_diag(l, mbd))
               for t, l in zip(t_mat, lpb)]
        t_mat = [t + r_[0:C] for t, r_ in zip(t_mat, res)]
        l_pow = [r_[C:] for r_ in res]
    t_mat = [t + _dot(t.astype(BF16), _block_diag(l.astype(BF16), mbd))
             for t, l in zip(t_mat, l_pow)]
    ub = [_dot(t.astype(BF16), _block_diag((p_[0:C] + a_[0:C]).astype(BF16), mbd)).astype(BF16)
          for t, p_, a_ in zip(t_mat, p, av)]
    o_part = [p_[C:] + a_[C:] + _dot(q, _block_diag(u, mbd))
              for p_, a_, q, u in zip(p, av, a_qb, ub)]
    for (b, g), s, v_, u, k_, b_ in zip(chains, s_sc, vg, ub, ktg, btg):
        ds = _dot_tn(jnp.concatenate([v_, u], axis=0), jnp.concatenate([k_, b_], axis=0))
        state_ref[b, g] = (s + ds) * mbdf * e_end[b][:, cols(g)]

    o = jnp.concatenate(
        [jnp.concatenate(o_part[b * N_GROUPS:(b + 1) * N_GROUPS], axis=1) for b in range(n_batch)],
        axis=0)
    ones = ones_ref[...]
    mu = _dot(o.astype(BF16), ones) * (1.0 / HEAD_DIM)
    oc = o - mu
    var = _dot((oc * oc).astype(BF16), ones) * (1.0 / HEAD_DIM)
    gg = gg_ref[...].reshape(n_batch * C, BRANCH_W)
    c2 = c2_ref[...].reshape(n_batch * C, BRANCH_W)
    o_ref[...] = (oc * lax.rsqrt(var + GN_EPS) * gg + c2).reshape(n_batch, C, BRANCH_W)


def _wkv_consts():
    C = WKV_CHUNK
    t = jnp.arange(C)[:, None]
    col = jnp.arange(GROUP_W)[None, :]
    s = col % C
    m_strict = (t > s).astype(F32)
    m_incl = (t >= s).astype(F32)
    eye = (t == s).astype(F32)
    rows = jnp.arange(GROUP_W)[:, None]
    mbd = (rows // C == col // HEAD_DIM)
    hh = jnp.arange(BRANCH_W) // HEAD_DIM
    ones = (hh[:, None] == hh[None, :]).astype(BF16)
    return m_strict, m_incl, eye, mbd.astype(BF16), mbd.astype(F32), ones


def _wkv_call(n_batch, seq, r, k, v, ka, bb, cum, gg, c2):
    C = WKV_CHUNK
    blk = pl.BlockSpec((n_batch, C, BRANCH_W), lambda i: (0, i, 0))
    consts = _wkv_consts()
    view = lambda a: a.reshape(n_batch, seq, BRANCH_W)
    out = pl.pallas_call(
        _wkv_kernel,
        grid=(seq // C,),
        in_specs=[blk] * 8 + [_const_spec(c.shape) for c in consts],
        out_specs=blk,
        out_shape=jax.ShapeDtypeStruct((n_batch, seq, BRANCH_W), F32),
        scratch_shapes=[pltpu.VMEM((n_batch, N_GROUPS, GROUP_W, GROUP_W), F32)],
        compiler_params=_params(1),
        name="wkv",
    )(*[view(a) for a in (r, k, v, ka, bb, cum, gg, c2)], *consts)
    return out.reshape(n_batch * seq, BRANCH_W)


def _ffn_weights(w1, w3, w2):
    up = lambda w: w.astype(BF16).reshape(D_MODEL, N_FF_CHUNKS, FF_CHUNK).transpose(1, 0, 2)
    return up(w1), up(w3), w2.astype(BF16).reshape(N_FF_CHUNKS, FF_CHUNK, D_MODEL)


def _row(p):
    return p.reshape(1, -1).astype(F32)


def _pad_rows(w, top, total):
    out = jnp.zeros((total, w.shape[1]), w.dtype)
    return lax.dynamic_update_slice(out, w, (top, 0))


def _block_diag_heads(w):
    h, n, m = w.shape
    eye = jnp.eye(h, dtype=w.dtype)
    return (eye[:, None, :, None] * w[:, :, None, :]).reshape(h * n, h * m)


def kernel(x, ln_g, ln_b, ffn_w1, ffn_w3, ffn_w2, w_in, gate_b, p_branch, w_out, rwkv_mu, rwkv_w0, rwkv_w2, rwkv_a0, rwkv_a2, rwkv_g2, rwkv_k_k, rwkv_k_a, rwkv_r_k, rwkv_gn_g, rwkv_gn_b, rwkv_v0, rwkv_v1, rwkv_v2, gmlp_ln_g, gmlp_ln_b, gmlp_ws, gmlp_sb, lru_conv_w, lru_conv_b, lru_wa, lru_ba, lru_wx, lru_bx, lru_lam):
    n_batch, seq, d_model = x.shape
    assert d_model == D_MODEL
    seq_tile = min(SEQ_TILE, seq)
    assert seq % seq_tile == 0 and seq_tile % GMLP_CHUNK == 0 and seq_tile % WKV_CHUNK == 0
    assert (n_batch * seq) % min(FFN_TILE, n_batch * seq) == 0
    depth = ln_g.shape[0]
    hh = jnp.arange(BRANCH_W) // HEAD_DIM
    head_ones = (hh[:, None] == hh[None, :]).astype(BF16)
    causal = jnp.tril(jnp.ones((GMLP_CHUNK, GMLP_CHUNK), F32))

    cur = x.reshape(n_batch * seq, d_model)
    v_first = None
    for l in range(depth):
        cur, cur_b = _ffn_call(cur, _ffn_weights(ffn_w1[l, 0], ffn_w3[l, 0], ffn_w2[l, 0]),
                               _row(ln_g[l, 0]), _row(ln_b[l, 0]))

        win = w_in[l].astype(BF16)
        rwkv_params = (
            win[:, :RWKV_COLS], _row(rwkv_mu[l]), _row(rwkv_w0[l]),
            _pad_rows(rwkv_w2[l].astype(BF16), 0, 128), _row(rwkv_a0[l]),
            _pad_rows(rwkv_a2[l].astype(BF16), 64, 128), rwkv_g2[l].astype(BF16),
            _row(rwkv_k_k[l]), _row(rwkv_k_a[l]), _row(rwkv_r_k[l]),
            _row(rwkv_gn_g[l]), _row(rwkv_gn_b[l]), head_ones)
        if l == 0:
            vres = None
        else:
            v1p = jnp.zeros((BRANCH_W, 128), BF16).at[:, :rwkv_v1.shape[2]].set(
                rwkv_v1[l - 1].astype(BF16))
            vres = (v_first, _row(rwkv_v0[l - 1]), v1p,
                    _pad_rows(rwkv_v2[l - 1].astype(BF16), 0, 128))
        outs = _rwkv_in_call(cur_b, n_batch, seq, rwkv_params, vres)
        if l == 0:
            v_first = outs[8]
        o_rwkv = _wkv_call(n_batch, seq, *outs[:8])

        sb_tile = jnp.tile(jnp.repeat(gmlp_sb[l].T, GMLP_GROUP_DIM, axis=1),
                           (seq_tile // GMLP_CHUNK, 1))
        mix_params = (
            win[:, 1792:2304], win[:, 2304:2816], win[:, 2816:3328], win[:, 3328:3840],
            win[:, 3840:].reshape(D_MODEL, 3, D_MODEL).transpose(1, 0, 2), gate_b[l].astype(F32),
            _row(gmlp_ln_g[l]), _row(gmlp_ln_b[l]), (gmlp_ws[l] * causal).astype(BF16),
            sb_tile.astype(F32), lru_conv_w[l].astype(F32), _row(lru_conv_b[l]),
            _block_diag_heads(lru_wa[l]).astype(BF16), _row(lru_ba[l]),
            _block_diag_heads(lru_wx[l]).astype(BF16), _row(lru_bx[l]), _row(lru_lam[l]),
            p_branch[l, 1].astype(BF16), p_branch[l, 2].astype(BF16))
        part, g0 = _mix_in_call(cur_b, n_batch, seq, mix_params)

        cur = _merge_ffn_call(
            cur, o_rwkv, part, g0, p_branch[l, 0].astype(BF16), w_out[l].astype(BF16),
            _row(ln_g[l, 1]), _row(ln_b[l, 1]),
            _ffn_weights(ffn_w1[l, 1], ffn_w3[l, 1], ffn_w2[l, 1]),
            _row(ln_g[l, 2]), _row(ln_b[l, 2]))
    return cur.reshape(n_batch, seq, d_model)
```

```python
import functools

import jax
import jax.numpy as jnp
from jax import lax
from jax.experimental import pallas as pl
from jax.experimental.pallas import tpu as pltpu

F32 = jnp.float32
BF16 = jnp.bfloat16

D_MODEL = 1024
D_FF = 2816
FF_CHUNK = 256
N_FF_CHUNKS = D_FF // FF_CHUNK
BRANCH_W = 512
HEAD_DIM = 64
N_HEADS = BRANCH_W // HEAD_DIM
RWKV_COLS = 1792
GMLP_GROUPS = 4
GMLP_GROUP_DIM = 128
GMLP_CHUNK = 128
WKV_CHUNK = 64
HEADS_PER_GROUP = 4
GROUP_W = HEADS_PER_GROUP * HEAD_DIM
N_GROUPS = N_HEADS // HEADS_PER_GROUP
CONV_WIDTH = 4
SUBLANES = 8
LN_EPS = 1e-5
GN_EPS = 64e-5
LRU_C = 8.0
DEPTH = 2
ALPHA = (2 * DEPTH) ** 0.25
FFN_TILE = 1024
MERGE_TILE = 512
SEQ_TILE = 512
VMEM_LIMIT = 56 * 1024 * 1024


def _dot(a, b):
    return jnp.dot(a, b, preferred_element_type=F32)


def _dot_nt(a, b):
    return lax.dot_general(a, b, (((1,), (1,)), ((), ())), preferred_element_type=F32)


def _dot_tn(a, b):
    return lax.dot_general(a, b, (((0,), (0,)), ((), ())), preferred_element_type=F32)


def _sigmoid(x):
    return 1.0 / (1.0 + jnp.exp(-x))


def _softplus(x):
    return jnp.maximum(x, 0.0) + jnp.log(1.0 + jnp.exp(-jnp.abs(x)))


def _gelu_tanh(x):
    return 0.5 * x * (1.0 + jnp.tanh(0.7978845608028654 * (x + 0.044715 * (x * x * x))))


def _layer_norm(z, g, b, eps):
    mu = jnp.mean(z, axis=-1, keepdims=True)
    zc = z - mu
    var = jnp.mean(zc * zc, axis=-1, keepdims=True)
    return zc * lax.rsqrt(var + eps) * g + b


def _const_spec(shape):
    nd = len(shape)
    return pl.BlockSpec(shape, lambda *_: (0,) * nd)


def _params(n_grid_axes):
    return pltpu.CompilerParams(
        dimension_semantics=("arbitrary",) * n_grid_axes, vmem_limit_bytes=VMEM_LIMIT)


def _swiglu_ln(x, xb_ref, h_ref, w1_ref, w3_ref, w2_ref, g, b):
    xb_ref[...] = x.astype(BF16)
    for j in range(N_FF_CHUNKS):
        cols = slice(j * FF_CHUNK, (j + 1) * FF_CHUNK)
        xb = xb_ref[...]
        a = _dot(xb, w1_ref[:, cols])
        c = _dot(xb, w3_ref[:, cols])
        h_ref[:, cols] = ((a * _sigmoid(a)) * c).astype(BF16)
    y = _dot(h_ref[...], w2_ref[...])
    return _layer_norm(ALPHA * x + 0.5 * y, g, b, LN_EPS)


def _ffn_kernel(x_ref, w1_ref, w3_ref, w2_ref, g_ref, b_ref, o_ref, ob_ref, xb_ref, h_ref):
    out = _swiglu_ln(x_ref[...], xb_ref, h_ref, w1_ref, w3_ref, w2_ref, g_ref[...], b_ref[...])
    o_ref[...] = out
    ob_ref[...] = out.astype(BF16)


def _merge_ffn_kernel(x_ref, o_ref, part_ref, g0_ref, p0_ref, wout_ref, g1_ref, b1_ref,
                      w1_ref, w3_ref, w2_ref, g2_ref, b2_ref, out_ref, xb_ref, h_ref):
    m0 = _dot(o_ref[...].astype(BF16), p0_ref[...])
    merged = part_ref[...].astype(F32) + g0_ref[...].astype(F32) * m0
    y = _dot(merged.astype(BF16), wout_ref[...])
    x2 = _layer_norm(ALPHA * x_ref[...] + y, g1_ref[...], b1_ref[...], LN_EPS)
    out_ref[...] = _swiglu_ln(x2, xb_ref, h_ref, w1_ref, w3_ref, w2_ref, g2_ref[...], b2_ref[...])


def _rows_spec(tm, width):
    return pl.BlockSpec((tm, width), lambda i: (i, 0))


def _ffn_weight_specs():
    return [
        _const_spec((D_MODEL, D_FF)),
        _const_spec((D_MODEL, D_FF)),
        _const_spec((D_FF, D_MODEL)),
    ]


def _ffn_call(x, ffn_w, ln_g, ln_b):
    n_rows = x.shape[0]
    tm = min(FFN_TILE, n_rows)
    vec = _const_spec((1, D_MODEL))
    return pl.pallas_call(
        _ffn_kernel,
        grid=(n_rows // tm,),
        in_specs=[_rows_spec(tm, D_MODEL)] + _ffn_weight_specs() + [vec, vec],
        out_specs=[_rows_spec(tm, D_MODEL)] * 2,
        out_shape=[jax.ShapeDtypeStruct((n_rows, D_MODEL), F32),
                   jax.ShapeDtypeStruct((n_rows, D_MODEL), BF16)],
        scratch_shapes=[pltpu.VMEM((tm, D_MODEL), BF16), pltpu.VMEM((tm, D_FF), BF16)],
        compiler_params=_params(1),
        name="ffn",
    )(x, *ffn_w, ln_g, ln_b)


def _merge_ffn_call(x, o_rwkv, part, g0, p0, w_out, ln_g1, ln_b1, ffn_w, ln_g2, ln_b2):
    n_rows = x.shape[0]
    tm = min(MERGE_TILE, n_rows)
    vec = _const_spec((1, D_MODEL))
    return pl.pallas_call(
        _merge_ffn_kernel,
        grid=(n_rows // tm,),
        in_specs=[
            _rows_spec(tm, D_MODEL), _rows_spec(tm, BRANCH_W), _rows_spec(tm, D_MODEL),
            _rows_spec(tm, D_MODEL), _const_spec((BRANCH_W, D_MODEL)),
            _const_spec((D_MODEL, D_MODEL)), vec, vec,
        ] + _ffn_weight_specs() + [vec, vec],
        out_specs=_rows_spec(tm, D_MODEL),
        out_shape=jax.ShapeDtypeStruct((n_rows, D_MODEL), F32),
        scratch_shapes=[pltpu.VMEM((tm, D_MODEL), BF16), pltpu.VMEM((tm, D_FF), BF16)],
        compiler_params=_params(1),
        name="merge_ffn",
    )(x, o_rwkv, part, g0, p0, w_out, ln_g1, ln_b1, *ffn_w, ln_g2, ln_b2)


def _seq_spec(n_tiles, tm, width):
    return pl.BlockSpec((tm, width), lambda b, i: (b * n_tiles + i, 0))


def _rwkv_in_kernel(*refs, has_vres):
    if has_vres:
        (x_ref, win_ref, mu_ref, w0_ref, w2_ref, a0_ref, a2_ref, g2_ref, kk_ref, ka_ref, rk_ref,
         gng_ref, gnb_ref, ones_ref, tri_ref, vf_ref, v0_ref, v1_ref, v2_ref,
         r_out, k_out, v_out, ka_out, bb_out, cum_out, gg_out, c2_out, zbuf_ref) = refs
    else:
        (x_ref, win_ref, mu_ref, w0_ref, w2_ref, a0_ref, a2_ref, g2_ref, kk_ref, ka_ref, rk_ref,
         gng_ref, gnb_ref, ones_ref, tri_ref,
         r_out, k_out, v_out, ka_out, bb_out, cum_out, gg_out, c2_out, vraw_out, zbuf_ref) = refs
    tm = x_ref.shape[0]
    base = SUBLANES

    @pl.when(pl.program_id(1) == 0)
    def _():
        zbuf_ref[0:base, :] = jnp.zeros((base, RWKV_COLS), F32)

    zbuf_ref[base:base + tm, :] = _dot(x_ref[...], win_ref[...])
    z = zbuf_ref[base:base + tm, :]
    zp = zbuf_ref[base - 1:base - 1 + tm, :]
    zm = z + mu_ref[...] * (zp - z)
    zbuf_ref[base - 1:base, :] = zbuf_ref[base + tm - 1:base + tm, :]

    r = zm[:, 0:512]
    k = zm[:, 512:1024]
    v = zm[:, 1024:1536]
    zwa = zm[:, 1536:1664]
    zg = zm[:, 1664:1792]

    if has_vres:
        lora = _dot(_dot(v.astype(BF16), v1_ref[...]).astype(BF16), v2_ref[...])
        v = v + (vf_ref[...] - v) * _sigmoid(v0_ref[...] + lora)
    else:
        vraw_out[...] = v

    w_log = -_softplus(-(w0_ref[...] + _dot(jnp.tanh(zwa).astype(BF16), w2_ref[...]))) - 0.5
    lw = -jnp.exp(w_log)
    a = _sigmoid(a0_ref[...] + _dot(zwa.astype(BF16), a2_ref[...]))
    g = _dot(_sigmoid(zg).astype(BF16), g2_ref[...])

    ones = ones_ref[...]
    kk = k * kk_ref[...]
    ss = _dot((kk * kk).astype(BF16), ones)
    kk = kk / jnp.maximum(jnp.sqrt(ss), 1e-12)
    k2 = k * (1.0 + (a - 1.0) * ka_ref[...])
    bonus = _dot((r * k2 * rk_ref[...]).astype(BF16), ones) * v

    r_out[...] = r
    k_out[...] = k2
    v_out[...] = v
    ka_out[...] = kk * jnp.exp(-lw)
    bb_out[...] = kk * a
    gg_out[...] = gng_ref[...] * g
    c2_out[...] = (gnb_ref[...] + bonus) * g

    tri = tri_ref[...]
    hi = lw.astype(BF16)
    rem = lw - hi.astype(F32)
    mid = rem.astype(BF16)
    lo = (rem - mid.astype(F32)).astype(BF16)
    for c in range(tm // WKV_CHUNK):
        rows = slice(c * WKV_CHUNK, (c + 1) * WKV_CHUNK)
        cum_out[rows, :] = _dot(tri, hi[rows]) + _dot(tri, mid[rows]) + _dot(tri, lo[rows])


def _rwkv_in_call(x, n_batch, seq, params, v_first):
    tm = min(SEQ_TILE, seq)
    n_tiles = seq // tm
    n_rows = n_batch * seq
    has_vres = v_first is not None
    row = lambda w: _seq_spec(n_tiles, tm, w)
    vec = _const_spec((1, BRANCH_W))
    in_specs = [row(D_MODEL), _const_spec((D_MODEL, RWKV_COLS)), _const_spec((1, RWKV_COLS)),
                vec, _const_spec((128, BRANCH_W)), vec, _const_spec((128, BRANCH_W)),
                _const_spec((128, BRANCH_W)), vec, vec, vec, vec, vec,
                _const_spec((BRANCH_W, BRANCH_W)), _const_spec((WKV_CHUNK, WKV_CHUNK))]
    tri = jnp.tril(jnp.ones((WKV_CHUNK, WKV_CHUNK), BF16))
    args = [x] + list(params) + [tri]
    n_out = 8
    if has_vres:
        in_specs += [row(BRANCH_W), vec, _const_spec((BRANCH_W, 128)), _const_spec((128, BRANCH_W))]
        args += list(v_first)
    else:
        n_out = 9
    return pl.pallas_call(
        functools.partial(_rwkv_in_kernel, has_vres=has_vres),
        grid=(n_batch, n_tiles),
        in_specs=in_specs,
        out_specs=[row(BRANCH_W)] * n_out,
        out_shape=[jax.ShapeDtypeStruct((n_rows, BRANCH_W), F32)] * n_out,
        scratch_shapes=[pltpu.VMEM((tm + SUBLANES, RWKV_COLS), F32)],
        compiler_params=_params(2),
        name="rwkv_in_vres" if has_vres else "rwkv_in",
    )(*args)


def _mix_in_kernel(x_ref, wgu_ref, wgv_ref, wlx_ref, wly_ref, wgate_ref, gateb_ref,
                   lng_ref, lnb_ref, ws_ref, sb_ref, convw_ref, convb_ref, wa_ref, ba_ref,
                   wx_ref, bx_ref, lam_ref, p1_ref, p2_ref,
                   part_out, g0_out, xbuf_ref, a_ref, b_ref, hcar_ref):
    tm = x_ref.shape[0]
    base = SUBLANES

    @pl.when(pl.program_id(1) == 0)
    def _():
        xbuf_ref[0:base, :] = jnp.zeros((base, BRANCH_W), F32)
        hcar_ref[...] = jnp.zeros_like(hcar_ref)

    xb = x_ref[...]

    u = _gelu_tanh(_dot(xb, wgu_ref[...]))
    vv = _layer_norm(_gelu_tanh(_dot(xb, wgv_ref[...])), lng_ref[...], lnb_ref[...], LN_EPS)
    vvb = vv.astype(BF16)
    s = jnp.concatenate([
        jnp.concatenate([
            _dot(ws_ref[g], vvb[c * GMLP_CHUNK:(c + 1) * GMLP_CHUNK,
                                g * GMLP_GROUP_DIM:(g + 1) * GMLP_GROUP_DIM])
            for g in range(GMLP_GROUPS)], axis=1)
        for c in range(tm // GMLP_CHUNK)], axis=0)
    o_gmlp = u * (s + sb_ref[...])

    xbuf_ref[base:base + tm, :] = _dot(xb, wlx_ref[...])
    xc = convb_ref[...]
    for j in range(CONV_WIDTH):
        off = base - (CONV_WIDTH - 1) + j
        xc = xc + xbuf_ref[off:off + tm, :] * convw_ref[j:j + 1, :]
    hist = CONV_WIDTH - 1
    xbuf_ref[base - hist:base, :] = xbuf_ref[base + tm - hist:base + tm, :]
    xcb = xc.astype(BF16)
    rg = _sigmoid(_dot(xcb, wa_ref[...]) + ba_ref[...])
    ig = _sigmoid(_dot(xcb, wx_ref[...]) + bx_ref[...])
    log_a = (-LRU_C) * rg * _softplus(-lam_ref[...])
    th = jnp.tanh(log_a)
    a = jnp.exp(log_a)
    bc = jnp.sqrt(-2.0 * th / (1.0 - th)) * (ig * xc)

    grouped = (tm // SUBLANES, SUBLANES, BRANCH_W)
    a = a.reshape(grouped)
    bc = bc.reshape(grouped)
    sub = lax.broadcasted_iota(jnp.int32, grouped, 1)
    for d in (1, 2, 4):
        keep = sub >= d
        a_sh = pltpu.roll(a, d, axis=1)
        b_sh = pltpu.roll(bc, d, axis=1)
        bc = jnp.where(keep, bc + a * b_sh, bc)
        a = jnp.where(keep, a * a_sh, a)
    a_ref[...] = a.reshape(tm, BRANCH_W)
    b_ref[...] = bc.reshape(tm, BRANCH_W)

    def step(q, hc):
        rows = pl.ds(pl.multiple_of(q * SUBLANES, SUBLANES), SUBLANES)
        h = b_ref[rows, :] + a_ref[rows, :] * hc
        b_ref[rows, :] = h
        return jnp.broadcast_to(h[SUBLANES - 1:SUBLANES, :], (SUBLANES, BRANCH_W))

    hcar_ref[...] = lax.fori_loop(0, tm // SUBLANES, step, hcar_ref[...], unroll=8)
    o_lru = _gelu_tanh(_dot(xb, wly_ref[...])) * b_ref[...]

    g1 = _sigmoid(_dot(xb, wgate_ref[1]) + gateb_ref[1:2, :])
    part = g1 * _dot(o_gmlp.astype(BF16), p1_ref[...])
    g2 = _sigmoid(_dot(xb, wgate_ref[2]) + gateb_ref[2:3, :])
    part_out[...] = (part + g2 * _dot(o_lru.astype(BF16), p2_ref[...])).astype(BF16)
    g0_out[...] = _sigmoid(_dot(xb, wgate_ref[0]) + gateb_ref[0:1, :]).astype(BF16)


def _mix_in_call(x, n_batch, seq, params):
    tm = min(SEQ_TILE, seq)
    n_tiles = seq // tm
    n_rows = n_batch * seq
    row = lambda w: _seq_spec(n_tiles, tm, w)
    vec = _const_spec((1, BRANCH_W))
    wcol = _const_spec((D_MODEL, BRANCH_W))
    sq = _const_spec((BRANCH_W, BRANCH_W))
    in_specs = [row(D_MODEL), wcol, wcol, wcol, wcol, _const_spec((3, D_MODEL, D_MODEL)),
                _const_spec((3, D_MODEL)), vec, vec,
                _const_spec((GMLP_GROUPS, GMLP_CHUNK, GMLP_CHUNK)),
                _const_spec((tm, BRANCH_W)), _const_spec((CONV_WIDTH, BRANCH_W)), vec,
                sq, vec, sq, vec, vec,
                _const_spec((BRANCH_W, D_MODEL)), _const_spec((BRANCH_W, D_MODEL))]
    return pl.pallas_call(
        _mix_in_kernel,
        grid=(n_batch, n_tiles),
        in_specs=in_specs,
        out_specs=[row(D_MODEL), row(D_MODEL)],
        out_shape=[jax.ShapeDtypeStruct((n_rows, D_MODEL), BF16)] * 2,
        scratch_shapes=[pltpu.VMEM((tm + SUBLANES, BRANCH_W), F32),
                        pltpu.VMEM((tm, BRANCH_W), F32), pltpu.VMEM((tm, BRANCH_W), F32),
                        pltpu.VMEM((SUBLANES, BRANCH_W), F32)],
        compiler_params=_params(2),
        name="mix_in",
    )(x, *params)


def _block_diag(y, mask):
    return jnp.concatenate([y] * HEADS_PER_GROUP, axis=0) * mask


def _wkv_kernel(r_ref, k_ref, v_ref, ka_ref, bb_ref, cum_ref, gg_ref, c2_ref,
                mstrict_ref, mincl_ref, eye_ref, mbd_ref, mbdf_ref, ones_ref,
                o_ref, state_ref):
    n_batch = r_ref.shape[0]
    C = WKV_CHUNK

    @pl.when(pl.program_id(0) == 0)
    def _():
        state_ref[...] = jnp.zeros_like(state_ref)

    m_strict, m_incl, eye = mstrict_ref[...], mincl_ref[...], eye_ref[...]
    mbd, mbdf = mbd_ref[...], mbdf_ref[...]

    cum = cum_ref[...]
    ref = cum[:, C // 2 - 1:C // 2, :]
    e_fwd = jnp.exp(cum - ref)
    e_bwd = jnp.exp(ref - cum)
    e_ref = jnp.exp(ref)
    e_end = jnp.exp(cum[:, C - 1:C, :] - ref)
    a_hat = (-ka_ref[...] * e_fwd).astype(BF16)
    r_hat = (r_ref[...] * e_fwd).astype(BF16)
    kt = (k_ref[...] * e_bwd).astype(BF16)
    bt = (bb_ref[...] * e_bwd).astype(BF16)
    vb = v_ref[...].astype(BF16)

    chains = [(b, g) for b in range(n_batch) for g in range(N_GROUPS)]
    cols = lambda g: slice(g * GROUP_W, (g + 1) * GROUP_W)
    pick = lambda arr: [arr[b][:, cols(g)] for b, g in chains]
    ktg, btg, vg = pick(kt), pick(bt), pick(vb)
    xg = [jnp.concatenate([ah, rh], axis=0) for ah, rh in zip(pick(a_hat), pick(r_hat))]

    sc = [_dot_nt(x, jnp.concatenate([_block_diag(b_, mbd), _block_diag(k_, mbd)], axis=0))
          for x, b_, k_ in zip(xg, btg, ktg)]
    s_sc = [state_ref[b, g] * e_ref[b][:, cols(g)] for b, g in chains]
    p = [_dot_nt(x, s.astype(BF16)) for x, s in zip(xg, s_sc)]
    l_mat = [s[0:C, 0:GROUP_W] * m_strict for s in sc]
    a_k = [jnp.concatenate([s[0:C, GROUP_W:] * m_strict, s[C:, GROUP_W:] * m_incl],
                           axis=0).astype(BF16) for s in sc]
    a_qb = [(s[C:, 0:GROUP_W] * m_incl).astype(BF16) for s in sc]
    av = [_dot(a, _block_diag(v_, mbd)) for a, v_ in zip(a_k, vg)]

    lb = [l.astype(BF16) for l in l_mat]
    l_pow = [_dot(l, _block_diag(l, mbd)) for l in lb]
    t_mat = [eye + l for l in l_mat]
    for _ in range(4):
        lpb = [l.astype(BF16) for l in l_pow]
        res = [_dot(jnp.concatenate([t.astype(BF16), l], axis=0), _block_diag(l, mbd))
               for t, l in zip(t_mat, lpb)]
        t_mat = [t + r_[0:C] for t, r_ in zip(t_mat, res)]
        l_pow = [r_[C:] for r_ in res]
    t_mat = [t + _dot(t.astype(BF16), _block_diag(l.astype(BF16), mbd))
             for t, l in zip(t_mat, l_pow)]
    ub = [_dot(t.astype(BF16), _block_diag((p_[0:C] + a_[0:C]).astype(BF16), mbd)).astype(BF16)
          for t, p_, a_ in zip(t_mat, p, av)]
    o_part = [p_[C:] + a_[C:] + _dot(q, _block_diag(u, mbd))
              for p_, a_, q, u in zip(p, av, a_qb, ub)]
    for (b, g), s, v_, u, k_, b_ in zip(chains, s_sc, vg, ub, ktg, btg):
        ds = _dot_tn(jnp.concatenate([v_, u], axis=0), jnp.concatenate([k_, b_], axis=0))
        state_ref[b, g] = (s + ds) * mbdf * e_end[b][:, cols(g)]

    o = jnp.concatenate(
        [jnp.concatenate(o_part[b * N_GROUPS:(b + 1) * N_GROUPS], axis=1) for b in range(n_batch)],
        axis=0)
    ones = ones_ref[...]
    mu = _dot(o.astype(BF16), ones) * (1.0 / HEAD_DIM)
    oc = o - mu
    var = _dot((oc * oc).astype(BF16), ones) * (1.0 / HEAD_DIM)
    gg = gg_ref[...].reshape(n_batch * C, BRANCH_W)
    c2 = c2_ref[...].reshape(n_batch * C, BRANCH_W)
    o_ref[...] = (oc * lax.rsqrt(var + GN_EPS) * gg + c2).reshape(n_batch, C, BRANCH_W)


def _wkv_consts():
    C = WKV_CHUNK
    t = jnp.arange(C)[:, None]
    col = jnp.arange(GROUP_W)[None, :]
    s = col % C
    m_strict = (t > s).astype(F32)
    m_incl = (t >= s).astype(F32)
    eye = (t == s).astype(F32)
    rows = jnp.arange(GROUP_W)[:, None]
    mbd = (rows // C == col // HEAD_DIM)
    hh = jnp.arange(BRANCH_W) // HEAD_DIM
    ones = (hh[:, None] == hh[None, :]).astype(BF16)
    return m_strict, m_incl, eye, mbd.astype(BF16), mbd.astype(F32), ones


def _wkv_call(n_batch, seq, r, k, v, ka, bb, cum, gg, c2):
    C = WKV_CHUNK
    blk = pl.BlockSpec((n_batch, C, BRANCH_W), lambda i: (0, i, 0))
    consts = _wkv_consts()
    view = lambda a: a.reshape(n_batch, seq, BRANCH_W)
    out = pl.pallas_call(
        _wkv_kernel,
        grid=(seq // C,),
        in_specs=[blk] * 8 + [_const_spec(c.shape) for c in consts],
        out_specs=blk,
        out_shape=jax.ShapeDtypeStruct((n_batch, seq, BRANCH_W), F32),
        scratch_shapes=[pltpu.VMEM((n_batch, N_GROUPS, GROUP_W, GROUP_W), F32)],
        compiler_params=_params(1),
        name="wkv",
    )(*[view(a) for a in (r, k, v, ka, bb, cum, gg, c2)], *consts)
    return out.reshape(n_batch * seq, BRANCH_W)


def _ffn_weights(w1, w3, w2):
    return w1.astype(BF16), w3.astype(BF16), w2.astype(BF16)


def _row(p):
    return p.reshape(1, -1).astype(F32)


def _pad_rows(w, top, total):
    out = jnp.zeros((total, w.shape[1]), w.dtype)
    return lax.dynamic_update_slice(out, w, (top, 0))


def _block_diag_heads(w):
    h, n, m = w.shape
    eye = jnp.eye(h, dtype=w.dtype)
    return (eye[:, None, :, None] * w[:, :, None, :]).reshape(h * n, h * m)


def kernel(x, ln_g, ln_b, ffn_w1, ffn_w3, ffn_w2, w_in, gate_b, p_branch, w_out, rwkv_mu, rwkv_w0, rwkv_w2, rwkv_a0, rwkv_a2, rwkv_g2, rwkv_k_k, rwkv_k_a, rwkv_r_k, rwkv_gn_g, rwkv_gn_b, rwkv_v0, rwkv_v1, rwkv_v2, gmlp_ln_g, gmlp_ln_b, gmlp_ws, gmlp_sb, lru_conv_w, lru_conv_b, lru_wa, lru_ba, lru_wx, lru_bx, lru_lam):
    n_batch, seq, d_model = x.shape
    assert d_model == D_MODEL
    seq_tile = min(SEQ_TILE, seq)
    assert seq % seq_tile == 0 and seq_tile % GMLP_CHUNK == 0 and seq_tile % WKV_CHUNK == 0
    assert (n_batch * seq) % min(FFN_TILE, n_batch * seq) == 0
    depth = ln_g.shape[0]
    hh = jnp.arange(BRANCH_W) // HEAD_DIM
    head_ones = (hh[:, None] == hh[None, :]).astype(BF16)
    causal = jnp.tril(jnp.ones((GMLP_CHUNK, GMLP_CHUNK), F32))

    cur = x.reshape(n_batch * seq, d_model)
    v_first = None
    for l in range(depth):
        cur, cur_b = _ffn_call(cur, _ffn_weights(ffn_w1[l, 0], ffn_w3[l, 0], ffn_w2[l, 0]),
                               _row(ln_g[l, 0]), _row(ln_b[l, 0]))

        win = w_in[l].astype(BF16)
        rwkv_params = (
            win[:, :RWKV_COLS], _row(rwkv_mu[l]), _row(rwkv_w0[l]),
            _pad_rows(rwkv_w2[l].astype(BF16), 0, 128), _row(rwkv_a0[l]),
            _pad_rows(rwkv_a2[l].astype(BF16), 64, 128), rwkv_g2[l].astype(BF16),
            _row(rwkv_k_k[l]), _row(rwkv_k_a[l]), _row(rwkv_r_k[l]),
            _row(rwkv_gn_g[l]), _row(rwkv_gn_b[l]), head_ones)
        if l == 0:
            vres = None
        else:
            v1p = jnp.zeros((BRANCH_W, 128), BF16).at[:, :rwkv_v1.shape[2]].set(
                rwkv_v1[l - 1].astype(BF16))
            vres = (v_first, _row(rwkv_v0[l - 1]), v1p,
                    _pad_rows(rwkv_v2[l - 1].astype(BF16), 0, 128))
        outs = _rwkv_in_call(cur_b, n_batch, seq, rwkv_params, vres)
        if l == 0:
            v_first = outs[8]
        o_rwkv = _wkv_call(n_batch, seq, *outs[:8])

        sb_tile = jnp.tile(jnp.repeat(gmlp_sb[l].T, GMLP_GROUP_DIM, axis=1),
                           (seq_tile // GMLP_CHUNK, 1))
        mix_params = (
            win[:, 1792:2304], win[:, 2304:2816], win[:, 2816:3328], win[:, 3328:3840],
            win[:, 3840:].reshape(D_MODEL, 3, D_MODEL).transpose(1, 0, 2), gate_b[l].astype(F32),
            _row(gmlp_ln_g[l]), _row(gmlp_ln_b[l]), (gmlp_ws[l] * causal).astype(BF16),
            sb_tile.astype(F32), lru_conv_w[l].astype(F32), _row(lru_conv_b[l]),
            _block_diag_heads(lru_wa[l]).astype(BF16), _row(lru_ba[l]),
            _block_diag_heads(lru_wx[l]).astype(BF16), _row(lru_bx[l]), _row(lru_lam[l]),
            p_branch[l, 1].astype(BF16), p_branch[l, 2].astype(BF16))
        part, g0 = _mix_in_call(cur_b, n_batch, seq, mix_params)

        cur = _merge_ffn_call(
            cur, o_rwkv, part, g0, p_branch[l, 0].astype(BF16), w_out[l].astype(BF16),
            _row(ln_g[l, 1]), _row(ln_b[l, 1]),
            _ffn_weights(ffn_w1[l, 1], ffn_w3[l, 1], ffn_w2[l, 1]),
            _row(ln_g[l, 2]), _row(ln_b[l, 2]))
    return cur.reshape(n_batch, seq, d_model)
```

```python
import functools

import jax
import jax.numpy as jnp
from jax import lax
from jax.experimental import pallas as pl
from jax.experimental.pallas import tpu as pltpu

F32 = jnp.float32
BF16 = jnp.bfloat16

D_MODEL = 1024
D_FF = 2816
FF_CHUNK = 256
N_FF_CHUNKS = D_FF // FF_CHUNK
BRANCH_W = 512
HEAD_DIM = 64
N_HEADS = BRANCH_W // HEAD_DIM
RWKV_COLS = 1792
GMLP_GROUPS = 4
GMLP_GROUP_DIM = 128
GMLP_CHUNK = 128
WKV_CHUNK = 64
HEADS_PER_GROUP = 4
GROUP_W = HEADS_PER_GROUP * HEAD_DIM
N_GROUPS = N_HEADS // HEADS_PER_GROUP
CONV_WIDTH = 4
SUBLANES = 8
LN_EPS = 1e-5
GN_EPS = 64e-5
LRU_C = 8.0
DEPTH = 2
ALPHA = (2 * DEPTH) ** 0.25
FFN_TILE = 1024
MERGE_TILE = 512
ROW_SPLIT = 2
SEQ_TILE = 512
VMEM_LIMIT = 56 * 1024 * 1024


def _dot(a, b):
    return jnp.dot(a, b, preferred_element_type=F32)


def _dot_nt(a, b):
    return lax.dot_general(a, b, (((1,), (1,)), ((), ())), preferred_element_type=F32)


def _dot_tn(a, b):
    return lax.dot_general(a, b, (((0,), (0,)), ((), ())), preferred_element_type=F32)


def _sigmoid(x):
    return 1.0 / (1.0 + jnp.exp(-x))


def _softplus(x):
    return jnp.maximum(x, 0.0) + jnp.log(1.0 + jnp.exp(-jnp.abs(x)))


def _gelu_tanh(x):
    return 0.5 * x * (1.0 + jnp.tanh(0.7978845608028654 * (x + 0.044715 * (x * x * x))))


def _layer_norm(z, g, b, eps):
    mu = jnp.mean(z, axis=-1, keepdims=True)
    zc = z - mu
    var = jnp.mean(zc * zc, axis=-1, keepdims=True)
    return zc * lax.rsqrt(var + eps) * g + b


def _const_spec(shape):
    nd = len(shape)
    return pl.BlockSpec(shape, lambda *_: (0,) * nd)


def _params(n_grid_axes):
    return pltpu.CompilerParams(
        dimension_semantics=("arbitrary",) * n_grid_axes, vmem_limit_bytes=VMEM_LIMIT)


def _swiglu_ln(x_parts, xb_ref, h_ref, w1_ref, w3_ref, w2_ref, g, b, emit):
    rb = x_parts[0].shape[0]
    for s, x in enumerate(x_parts):
        xb_ref[s * rb:(s + 1) * rb, :] = x.astype(BF16)
    for j in range(N_FF_CHUNKS):
        cols = slice(j * FF_CHUNK, (j + 1) * FF_CHUNK)
        xb = xb_ref[...]
        a = _dot(xb, w1_ref[:, cols])
        c = _dot(xb, w3_ref[:, cols])
        h_ref[:, cols] = ((a * _sigmoid(a)) * c).astype(BF16)
    for s, x in enumerate(x_parts):
        rows = slice(s * rb, (s + 1) * rb)
        y = _dot(h_ref[rows, :], w2_ref[...])
        emit(rows, _layer_norm(ALPHA * x + 0.5 * y, g, b, LN_EPS))


def _ffn_kernel(x_ref, w1_ref, w3_ref, w2_ref, g_ref, b_ref, o_ref, ob_ref, xb_ref, h_ref):
    rb = x_ref.shape[0] // ROW_SPLIT

    def emit(rows, out):
        o_ref[rows, :] = out
        ob_ref[rows, :] = out.astype(BF16)

    _swiglu_ln([x_ref[s * rb:(s + 1) * rb, :] for s in range(ROW_SPLIT)], xb_ref, h_ref,
               w1_ref, w3_ref, w2_ref, g_ref[...], b_ref[...], emit)


def _merge_ffn_kernel(x_ref, o_ref, part_ref, g0_ref, p0_ref, wout_ref, g1_ref, b1_ref,
                      w1_ref, w3_ref, w2_ref, g2_ref, b2_ref, out_ref, xb_ref, h_ref):
    rb = x_ref.shape[0] // ROW_SPLIT
    x2 = []
    for s in range(ROW_SPLIT):
        rows = slice(s * rb, (s + 1) * rb)
        m0 = _dot(o_ref[rows, :].astype(BF16), p0_ref[...])
        merged = part_ref[rows, :].astype(F32) + g0_ref[rows, :].astype(F32) * m0
        y = _dot(merged.astype(BF16), wout_ref[...])
        x2.append(_layer_norm(ALPHA * x_ref[rows, :] + y, g1_ref[...], b1_ref[...], LN_EPS))

    def emit(rows, out):
        out_ref[rows, :] = out

    _swiglu_ln(x2, xb_ref, h_ref, w1_ref, w3_ref, w2_ref, g2_ref[...], b2_ref[...], emit)


def _rows_spec(tm, width):
    return pl.BlockSpec((tm, width), lambda i: (i, 0))


def _ffn_weight_specs():
    return [
        _const_spec((D_MODEL, D_FF)),
        _const_spec((D_MODEL, D_FF)),
        _const_spec((D_FF, D_MODEL)),
    ]


def _ffn_call(x, ffn_w, ln_g, ln_b):
    n_rows = x.shape[0]
    tm = min(FFN_TILE, n_rows)
    vec = _const_spec((1, D_MODEL))
    return pl.pallas_call(
        _ffn_kernel,
        grid=(n_rows // tm,),
        in_specs=[_rows_spec(tm, D_MODEL)] + _ffn_weight_specs() + [vec, vec],
        out_specs=[_rows_spec(tm, D_MODEL)] * 2,
        out_shape=[jax.ShapeDtypeStruct((n_rows, D_MODEL), F32),
                   jax.ShapeDtypeStruct((n_rows, D_MODEL), BF16)],
        scratch_shapes=[pltpu.VMEM((tm, D_MODEL), BF16), pltpu.VMEM((tm, D_FF), BF16)],
        compiler_params=_params(1),
        name="ffn",
    )(x, *ffn_w, ln_g, ln_b)


def _merge_ffn_call(x, o_rwkv, part, g0, p0, w_out, ln_g1, ln_b1, ffn_w, ln_g2, ln_b2):
    n_rows = x.shape[0]
    tm = min(MERGE_TILE, n_rows)
    vec = _const_spec((1, D_MODEL))
    return pl.pallas_call(
        _merge_ffn_kernel,
        grid=(n_rows // tm,),
        in_specs=[
            _rows_spec(tm, D_MODEL), _rows_spec(tm, BRANCH_W), _rows_spec(tm, D_MODEL),
            _rows_spec(tm, D_MODEL), _const_spec((BRANCH_W, D_MODEL)),
            _const_spec((D_MODEL, D_MODEL)), vec, vec,
        ] + _ffn_weight_specs() + [vec, vec],
        out_specs=_rows_spec(tm, D_MODEL),
        out_shape=jax.ShapeDtypeStruct((n_rows, D_MODEL), F32),
        scratch_shapes=[pltpu.VMEM((tm, D_MODEL), BF16), pltpu.VMEM((tm, D_FF), BF16)],
        compiler_params=_params(1),
        name="merge_ffn",
    )(x, o_rwkv, part, g0, p0, w_out, ln_g1, ln_b1, *ffn_w, ln_g2, ln_b2)


def _seq_spec(n_tiles, tm, width):
    return pl.BlockSpec((tm, width), lambda b, i: (b * n_tiles + i, 0))


def _rwkv_in_kernel(*refs, has_vres):
    if has_vres:
        (x_ref, win_ref, mu_ref, w0_ref, w2_ref, a0_ref, a2_ref, g2_ref, kk_ref, ka_ref, rk_ref,
         gng_ref, gnb_ref, ones_ref, tri_ref, vf_ref, v0_ref, v1_ref, v2_ref,
         r_out, k_out, v_out, ka_out, bb_out, cum_out, gg_out, c2_out, zbuf_ref) = refs
    else:
        (x_ref, win_ref, mu_ref, w0_ref, w2_ref, a0_ref, a2_ref, g2_ref, kk_ref, ka_ref, rk_ref,
         gng_ref, gnb_ref, ones_ref, tri_ref,
         r_out, k_out, v_out, ka_out, bb_out, cum_out, gg_out, c2_out, vraw_out, zbuf_ref) = refs
    tm = x_ref.shape[0]
    base = SUBLANES

    @pl.when(pl.program_id(1) == 0)
    def _():
        zbuf_ref[0:base, :] = jnp.zeros((base, RWKV_COLS), F32)

    xb = x_ref[...]

    def project(cols):
        zbuf_ref[base:base + tm, cols] = _dot(xb, win_ref[:, cols])

    def shifted(cols):
        z = zbuf_ref[base:base + tm, cols]
        zp = zbuf_ref[base - 1:base - 1 + tm, cols]
        return z + mu_ref[:, cols] * (zp - z)

    project(slice(1536, 1792))
    project(slice(512, 1024))
    zwa = shifted(slice(1536, 1664))
    zg = shifted(slice(1664, 1792))
    pre_w = _dot(jnp.tanh(zwa).astype(BF16), w2_ref[...])
    pre_a = _dot(zwa.astype(BF16), a2_ref[...])
    g = _dot(_sigmoid(zg).astype(BF16), g2_ref[...])
    project(slice(0, 512))

    w_log = -_softplus(-(w0_ref[...] + pre_w)) - 0.5
    lw = -jnp.exp(w_log)
    a = _sigmoid(a0_ref[...] + pre_a)
    k = shifted(slice(512, 1024))
    ones = ones_ref[...]
    kk = k * kk_ref[...]
    ss = _dot((kk * kk).astype(BF16), ones)
    project(slice(1024, 1536))

    tri = tri_ref[...]
    hi = lw.astype(BF16)
    rem = lw - hi.astype(F32)
    mid = rem.astype(BF16)
    lo = (rem - mid.astype(F32)).astype(BF16)
    for c in range(tm // WKV_CHUNK):
        rows = slice(c * WKV_CHUNK, (c + 1) * WKV_CHUNK)
        cum_out[rows, :] = _dot(tri, hi[rows]) + _dot(tri, mid[rows]) + _dot(tri, lo[rows])

    kk = kk * jnp.minimum(lax.rsqrt(ss), 1e12)
    k2 = k * (1.0 + (a - 1.0) * ka_ref[...])
    r = shifted(slice(0, 512))
    rk_sum = _dot((r * k2 * rk_ref[...]).astype(BF16), ones)
    r_out[...] = r
    k_out[...] = k2
    ka_out[...] = kk * jnp.exp(-lw)
    bb_out[...] = kk * a
    gg_out[...] = gng_ref[...] * g

    v = shifted(slice(1024, 1536))
    if has_vres:
        lora = _dot(_dot(v.astype(BF16), v1_ref[...]).astype(BF16), v2_ref[...])
        v = v + (vf_ref[...] - v) * _sigmoid(v0_ref[...] + lora)
    else:
        vraw_out[...] = v
    v_out[...] = v
    c2_out[...] = (gnb_ref[...] + rk_sum * v) * g
    zbuf_ref[base - 1:base, :] = zbuf_ref[base + tm - 1:base + tm, :]


def _rwkv_in_call(x, n_batch, seq, params, v_first):
    tm = min(SEQ_TILE, seq)
    n_tiles = seq // tm
    n_rows = n_batch * seq
    has_vres = v_first is not None
    row = lambda w: _seq_spec(n_tiles, tm, w)
    vec = _const_spec((1, BRANCH_W))
    in_specs = [row(D_MODEL), _const_spec((D_MODEL, RWKV_COLS)), _const_spec((1, RWKV_COLS)),
                vec, _const_spec((128, BRANCH_W)), vec, _const_spec((128, BRANCH_W)),
                _const_spec((128, BRANCH_W)), vec, vec, vec, vec, vec,
                _const_spec((BRANCH_W, BRANCH_W)), _const_spec((WKV_CHUNK, WKV_CHUNK))]
    tri = jnp.tril(jnp.ones((WKV_CHUNK, WKV_CHUNK), BF16))
    args = [x] + list(params) + [tri]
    n_out = 8
    if has_vres:
        in_specs += [row(BRANCH_W), vec, _const_spec((BRANCH_W, 128)), _const_spec((128, BRANCH_W))]
        args += list(v_first)
    else:
        n_out = 9
    return pl.pallas_call(
        functools.partial(_rwkv_in_kernel, has_vres=has_vres),
        grid=(n_batch, n_tiles),
        in_specs=in_specs,
        out_specs=[row(BRANCH_W)] * n_out,
        out_shape=[jax.ShapeDtypeStruct((n_rows, BRANCH_W), F32)] * n_out,
        scratch_shapes=[pltpu.VMEM((tm + SUBLANES, RWKV_COLS), F32)],
        compiler_params=_params(2),
        name="rwkv_in_vres" if has_vres else "rwkv_in",
    )(*args)


def _mix_in_kernel(x_ref, wgu_ref, wgv_ref, wlx_ref, wly_ref, wgate_ref, gateb_ref,
                   lng_ref, lnb_ref, ws_ref, sb_ref, convw_ref, convb_ref, wa_ref, ba_ref,
                   wx_ref, bx_ref, lam_ref, p1_ref, p2_ref,
                   part_out, g0_out, xbuf_ref, a_ref, b_ref, hcar_ref):
    tm = x_ref.shape[0]
    base = SUBLANES

    @pl.when(pl.program_id(1) == 0)
    def _():
        xbuf_ref[0:base, :] = jnp.zeros((base, BRANCH_W), F32)
        hcar_ref[...] = jnp.zeros_like(hcar_ref)

    xb = x_ref[...]
    gate_cols = [slice(j * GROUP_W, (j + 1) * GROUP_W) for j in range(D_MODEL // GROUP_W)]

    def gate(idx, cols):
        return _sigmoid(_dot(xb, wgate_ref[idx, :, cols]) + gateb_ref[idx:idx + 1, cols])

    xbuf_ref[base:base + tm, :] = _dot(xb, wlx_ref[...])
    zv = _dot(xb, wgv_ref[...])

    xc = convb_ref[...]
    for j in range(CONV_WIDTH):
        off = base - (CONV_WIDTH - 1) + j
        xc = xc + xbuf_ref[off:off + tm, :] * convw_ref[j:j + 1, :]
    hist = CONV_WIDTH - 1
    xbuf_ref[base - hist:base, :] = xbuf_ref[base + tm - hist:base + tm, :]
    xcb = xc.astype(BF16)
    pre_r = _dot(xcb, wa_ref[...])
    pre_i = _dot(xcb, wx_ref[...])

    vvb = _layer_norm(_gelu_tanh(zv), lng_ref[...], lnb_ref[...], LN_EPS).astype(BF16)
    zu = _dot(xb, wgu_ref[...])

    rg = _sigmoid(pre_r + ba_ref[...])
    ig = _sigmoid(pre_i + bx_ref[...])
    log_a = (-LRU_C) * rg * _softplus(-lam_ref[...])
    th = jnp.tanh(log_a)
    a = jnp.exp(log_a)
    bc = jnp.sqrt(-2.0 * th / (1.0 - th)) * (ig * xc)

    s = jnp.concatenate([
        jnp.concatenate([
            _dot(ws_ref[g], vvb[c * GMLP_CHUNK:(c + 1) * GMLP_CHUNK,
                                g * GMLP_GROUP_DIM:(g + 1) * GMLP_GROUP_DIM])
            for g in range(GMLP_GROUPS)], axis=1)
        for c in range(tm // GMLP_CHUNK)], axis=0)

    grouped = (tm // SUBLANES, SUBLANES, BRANCH_W)
    a = a.reshape(grouped)
    bc = bc.reshape(grouped)
    sub = lax.broadcasted_iota(jnp.int32, grouped, 1)
    for level, d in enumerate((1, 2, 4)):
        g0_out[:, gate_cols[level]] = gate(0, gate_cols[level]).astype(BF16)
        keep = sub >= d
        a_sh = pltpu.roll(a, d, axis=1)
        b_sh = pltpu.roll(bc, d, axis=1)
        bc = jnp.where(keep, bc + a * b_sh, bc)
        a = jnp.where(keep, a * a_sh, a)
    g0_out[:, gate_cols[3]] = gate(0, gate_cols[3]).astype(BF16)
    a_ref[...] = a.reshape(tm, BRANCH_W)
    b_ref[...] = bc.reshape(tm, BRANCH_W)
    zy = _dot(xb, wly_ref[...])
    o_gmlp = _gelu_tanh(zu) * (s + sb_ref[...])

    hc = hcar_ref[...]
    for q in range(tm // SUBLANES):
        rows = slice(q * SUBLANES, (q + 1) * SUBLANES)
        h = b_ref[rows, :] + a_ref[rows, :] * hc
        b_ref[rows, :] = h
        hc = jnp.broadcast_to(h[SUBLANES - 1:SUBLANES, :], (SUBLANES, BRANCH_W))
    hcar_ref[...] = hc

    m1 = _dot(o_gmlp.astype(BF16), p1_ref[...])
    y = _gelu_tanh(zy)
    part = jnp.concatenate([gate(1, cols) * m1[:, cols] for cols in gate_cols], axis=1)
    m2 = _dot((y * b_ref[...]).astype(BF16), p2_ref[...])
    for cols in gate_cols:
        part_out[:, cols] = (part[:, cols] + gate(2, cols) * m2[:, cols]).astype(BF16)


def _mix_in_call(x, n_batch, seq, params):
    tm = min(SEQ_TILE, seq)
    n_tiles = seq // tm
    n_rows = n_batch * seq
    row = lambda w: _seq_spec(n_tiles, tm, w)
    vec = _const_spec((1, BRANCH_W))
    wcol = _const_spec((D_MODEL, BRANCH_W))
    sq = _const_spec((BRANCH_W, BRANCH_W))
    in_specs = [row(D_MODEL), wcol, wcol, wcol, wcol, _const_spec((3, D_MODEL, D_MODEL)),
                _const_spec((3, D_MODEL)), vec, vec,
                _const_spec((GMLP_GROUPS, GMLP_CHUNK, GMLP_CHUNK)),
                _const_spec((tm, BRANCH_W)), _const_spec((CONV_WIDTH, BRANCH_W)), vec,
                sq, vec, sq, vec, vec,
                _const_spec((BRANCH_W, D_MODEL)), _const_spec((BRANCH_W, D_MODEL))]
    return pl.pallas_call(
        _mix_in_kernel,
        grid=(n_batch, n_tiles),
        in_specs=in_specs,
        out_specs=[row(D_MODEL), row(D_MODEL)],
        out_shape=[jax.ShapeDtypeStruct((n_rows, D_MODEL), BF16)] * 2,
        scratch_shapes=[pltpu.VMEM((tm + SUBLANES, BRANCH_W), F32),
                        pltpu.VMEM((tm, BRANCH_W), F32), pltpu.VMEM((tm, BRANCH_W), F32),
                        pltpu.VMEM((SUBLANES, BRANCH_W), F32)],
        compiler_params=_params(2),
        name="mix_in",
    )(x, *params)


def _block_diag(y, mask):
    return jnp.concatenate([y] * HEADS_PER_GROUP, axis=0) * mask


def _wkv_kernel(r_ref, k_ref, v_ref, ka_ref, bb_ref, cum_ref, gg_ref, c2_ref,
                mstrict_ref, mincl_ref, eye_ref, mbd_ref, mbdf_ref, ones_ref,
                o_ref, state_ref):
    n_batch = r_ref.shape[0]
    C = WKV_CHUNK

    @pl.when(pl.program_id(0) == 0)
    def _():
        state_ref[...] = jnp.zeros_like(state_ref)

    m_strict, m_incl, eye = mstrict_ref[...], mincl_ref[...], eye_ref[...]
    mbd, mbdf = mbd_ref[...], mbdf_ref[...]

    cum = cum_ref[...]
    ref = cum[:, C // 2 - 1:C // 2, :]
    e_fwd = jnp.exp(cum - ref)
    e_bwd = jnp.exp(ref - cum)
    e_ref = jnp.exp(ref)
    e_end = jnp.exp(cum[:, C - 1:C, :] - ref)
    a_hat = (-ka_ref[...] * e_fwd).astype(BF16)
    r_hat = (r_ref[...] * e_fwd).astype(BF16)
    kt = (k_ref[...] * e_bwd).astype(BF16)
    bt = (bb_ref[...] * e_bwd).astype(BF16)
    vb = v_ref[...].astype(BF16)

    chains = [(b, g) for b in range(n_batch) for g in range(N_GROUPS)]
    cols = lambda g: slice(g * GROUP_W, (g + 1) * GROUP_W)
    pick = lambda arr: [arr[b][:, cols(g)] for b, g in chains]
    ktg, btg, vg = pick(kt), pick(bt), pick(vb)
    xg = [jnp.concatenate([ah, rh], axis=0) for ah, rh in zip(pick(a_hat), pick(r_hat))]

    sc = [_dot_nt(x, jnp.concatenate([_block_diag(b_, mbd), _block_diag(k_, mbd)], axis=0))
          for x, b_, k_ in zip(xg, btg, ktg)]
    s_sc = [state_ref[b, g] * e_ref[b][:, cols(g)] for b, g in chains]
    p = [_dot_nt(x, s.astype(BF16)) for x, s in zip(xg, s_sc)]
    l_mat = [s[0:C, 0:GROUP_W] * m_strict for s in sc]
    a_k = [jnp.concatenate([s[0:C, GROUP_W:] * m_strict, s[C:, GROUP_W:] * m_incl],
                           axis=0).astype(BF16) for s in sc]
    a_qb = [(s[C:, 0:GROUP_W] * m_incl).astype(BF16) for s in sc]
    av = [_dot(a, _block_diag(v_, mbd)) for a, v_ in zip(a_k, vg)]

    lb = [l.astype(BF16) for l in l_mat]
    l_pow = [_dot(l, _block_diag(l, mbd)) for l in lb]
    t_mat = [eye + l for l in l_mat]
    for _ in range(4):
        lpb = [l.astype(BF16) for l in l_pow]
        res = [_dot(jnp.concatenate([t.astype(BF16), l], axis=0), _block_diag(l, mbd))
               for t, l in zip(t_mat, lpb)]
        t_mat = [t + r_[0:C] for t, r_ in zip(t_mat, res)]
        l_pow = [r_[C:] for r_ in res]
    t_mat = [t + _dot(t.astype(BF16), _block_diag(l.astype(BF16), mbd))
             for t, l in zip(t_mat, l_pow)]
    ub = [_dot(t.astype(BF16), _block_diag((p_[0:C] + a_[0:C]).astype(BF16), mbd)).astype(BF16)
          for t, p_, a_ in zip(t_mat, p, av)]
    o_part = [p_[C:] + a_[C:] + _dot(q, _block_diag(u, mbd))
              for p_, a_, q, u in zip(p, av, a_qb, ub)]
    for (b, g), s, v_, u, k_, b_ in zip(chains, s_sc, vg, ub, ktg, btg):
        ds = _dot_tn(jnp.concatenate([v_, u], axis=0), jnp.concatenate([k_, b_], axis=0))
        state_ref[b, g] = (s + ds) * mbdf * e_end[b][:, cols(g)]

    o = jnp.concatenate(
        [jnp.concatenate(o_part[b * N_GROUPS:(b + 1) * N_GROUPS], axis=1) for b in range(n_batch)],
        axis=0)
    ones = ones_ref[...]
    mu = _dot(o.astype(BF16), ones) * (1.0 / HEAD_DIM)
    oc = o - mu
    var = _dot((oc * oc).astype(BF16), ones) * (1.0 / HEAD_DIM)
    gg = gg_ref[...].reshape(n_batch * C, BRANCH_W)
    c2 = c2_ref[...].reshape(n_batch * C, BRANCH_W)
    o_ref[...] = (oc * lax.rsqrt(var + GN_EPS) * gg + c2).reshape(n_batch, C, BRANCH_W)


def _wkv_consts():
    C = WKV_CHUNK
    t = jnp.arange(C)[:, None]
    col = jnp.arange(GROUP_W)[None, :]
    s = col % C
    m_strict = (t > s).astype(F32)
    m_incl = (t >= s).astype(F32)
    eye = (t == s).astype(F32)
    rows = jnp.arange(GROUP_W)[:, None]
    mbd = (rows // C == col // HEAD_DIM)
    hh = jnp.arange(BRANCH_W) // HEAD_DIM
    ones = (hh[:, None] == hh[None, :]).astype(BF16)
    return m_strict, m_incl, eye, mbd.astype(BF16), mbd.astype(F32), ones


def _wkv_call(n_batch, seq, r, k, v, ka, bb, cum, gg, c2):
    C = WKV_CHUNK
    blk = pl.BlockSpec((n_batch, C, BRANCH_W), lambda i: (0, i, 0))
    consts = _wkv_consts()
    view = lambda a: a.reshape(n_batch, seq, BRANCH_W)
    out = pl.pallas_call(
        _wkv_kernel,
        grid=(seq // C,),
        in_specs=[blk] * 8 + [_const_spec(c.shape) for c in consts],
        out_specs=blk,
        out_shape=jax.ShapeDtypeStruct((n_batch, seq, BRANCH_W), F32),
        scratch_shapes=[pltpu.VMEM((n_batch, N_GROUPS, GROUP_W, GROUP_W), F32)],
        compiler_params=_params(1),
        name="wkv",
    )(*[view(a) for a in (r, k, v, ka, bb, cum, gg, c2)], *consts)
    return out.reshape(n_batch * seq, BRANCH_W)


def _ffn_weights(w1, w3, w2):
    return w1.astype(BF16), w3.astype(BF16), w2.astype(BF16)


def _row(p):
    return p.reshape(1, -1).astype(F32)


def _pad_rows(w, top, total):
    out = jnp.zeros((total, w.shape[1]), w.dtype)
    return lax.dynamic_update_slice(out, w, (top, 0))


def _block_diag_heads(w):
    h, n, m = w.shape
    eye = jnp.eye(h, dtype=w.dtype)
    return (eye[:, None, :, None] * w[:, :, None, :]).reshape(h * n, h * m)


def kernel(x, ln_g, ln_b, ffn_w1, ffn_w3, ffn_w2, w_in, gate_b, p_branch, w_out, rwkv_mu, rwkv_w0, rwkv_w2, rwkv_a0, rwkv_a2, rwkv_g2, rwkv_k_k, rwkv_k_a, rwkv_r_k, rwkv_gn_g, rwkv_gn_b, rwkv_v0, rwkv_v1, rwkv_v2, gmlp_ln_g, gmlp_ln_b, gmlp_ws, gmlp_sb, lru_conv_w, lru_conv_b, lru_wa, lru_ba, lru_wx, lru_bx, lru_lam):
    n_batch, seq, d_model = x.shape
    assert d_model == D_MODEL
    seq_tile = min(SEQ_TILE, seq)
    assert seq % seq_tile == 0 and seq_tile % GMLP_CHUNK == 0 and seq_tile % WKV_CHUNK == 0
    assert (n_batch * seq) % min(FFN_TILE, n_batch * seq) == 0
    depth = ln_g.shape[0]
    hh = jnp.arange(BRANCH_W) // HEAD_DIM
    head_ones = (hh[:, None] == hh[None, :]).astype(BF16)
    causal = jnp.tril(jnp.ones((GMLP_CHUNK, GMLP_CHUNK), F32))

    cur = x.reshape(n_batch * seq, d_model)
    v_first = None
    for l in range(depth):
        cur, cur_b = _ffn_call(cur, _ffn_weights(ffn_w1[l, 0], ffn_w3[l, 0], ffn_w2[l, 0]),
                               _row(ln_g[l, 0]), _row(ln_b[l, 0]))

        win = w_in[l].astype(BF16)
        rwkv_params = (
            win[:, :RWKV_COLS], _row(rwkv_mu[l]), _row(rwkv_w0[l]),
            _pad_rows(rwkv_w2[l].astype(BF16), 0, 128), _row(rwkv_a0[l]),
            _pad_rows(rwkv_a2[l].astype(BF16), 64, 128), rwkv_g2[l].astype(BF16),
            _row(rwkv_k_k[l]), _row(rwkv_k_a[l]), _row(rwkv_r_k[l]),
            _row(rwkv_gn_g[l]), _row(rwkv_gn_b[l]), head_ones)
        if l == 0:
            vres = None
        else:
            v1p = jnp.zeros((BRANCH_W, 128), BF16).at[:, :rwkv_v1.shape[2]].set(
                rwkv_v1[l - 1].astype(BF16))
            vres = (v_first, _row(rwkv_v0[l - 1]), v1p,
                    _pad_rows(rwkv_v2[l - 1].astype(BF16), 0, 128))
        outs = _rwkv_in_call(cur_b, n_batch, seq, rwkv_params, vres)
        if l == 0:
            v_first = outs[8]
        o_rwkv = _wkv_call(n_batch, seq, *outs[:8])

        sb_tile = jnp.tile(jnp.repeat(gmlp_sb[l].T, GMLP_GROUP_DIM, axis=1),
                           (seq_tile // GMLP_CHUNK, 1))
        mix_params = (
            win[:, 1792:2304], win[:, 2304:2816], win[:, 2816:3328], win[:, 3328:3840],
            win[:, 3840:].reshape(D_MODEL, 3, D_MODEL).transpose(1, 0, 2), gate_b[l].astype(F32),
            _row(gmlp_ln_g[l]), _row(gmlp_ln_b[l]), (gmlp_ws[l] * causal).astype(BF16),
            sb_tile.astype(F32), lru_conv_w[l].astype(F32), _row(lru_conv_b[l]),
            _block_diag_heads(lru_wa[l]).astype(BF16), _row(lru_ba[l]),
            _block_diag_heads(lru_wx[l]).astype(BF16), _row(lru_bx[l]), _row(lru_lam[l]),
            p_branch[l, 1].astype(BF16), p_branch[l, 2].astype(BF16))
        part, g0 = _mix_in_call(cur_b, n_batch, seq, mix_params)

        cur = _merge_ffn_call(
            cur, o_rwkv, part, g0, p_branch[l, 0].astype(BF16), w_out[l].astype(BF16),
            _row(ln_g[l, 1]), _row(ln_b[l, 1]),
            _ffn_weights(ffn_w1[l, 1], ffn_w3[l, 1], ffn_w2[l, 1]),
            _row(ln_g[l, 2]), _row(ln_b[l, 2]))
    return cur.reshape(n_batch, seq, d_model)
```

```python
import functools

import jax
import jax.numpy as jnp
from jax import lax
from jax.experimental import pallas as pl
from jax.experimental.pallas import tpu as pltpu

F32 = jnp.float32
BF16 = jnp.bfloat16

D_MODEL = 1024
D_FF = 2816
FF_CHUNK = 256
N_FF_CHUNKS = D_FF // FF_CHUNK
BRANCH_W = 512
HEAD_DIM = 64
N_HEADS = BRANCH_W // HEAD_DIM
RWKV_COLS = 1792
GMLP_GROUPS = 4
GMLP_GROUP_DIM = 128
GMLP_CHUNK = 128
WKV_CHUNK = 64
HEADS_PER_GROUP = 4
GROUP_W = HEADS_PER_GROUP * HEAD_DIM
N_GROUPS = N_HEADS // HEADS_PER_GROUP
CONV_WIDTH = 4
SUBLANES = 8
LN_EPS = 1e-5
GN_EPS = 64e-5
LRU_C = 8.0
DEPTH = 2
ALPHA = (2 * DEPTH) ** 0.25
FFN_TILE = 1024
MERGE_TILE = 512
FFN_ROW_SPLIT = 4
MERGE_ROW_SPLIT = 2
SEQ_TILE = 512
VMEM_LIMIT = 56 * 1024 * 1024


def _dot(a, b):
    return jnp.dot(a, b, preferred_element_type=F32)


def _dot_nt(a, b):
    return lax.dot_general(a, b, (((1,), (1,)), ((), ())), preferred_element_type=F32)


def _dot_tn(a, b):
    return lax.dot_general(a, b, (((0,), (0,)), ((), ())), preferred_element_type=F32)


def _dot_group_diag(x, w_ref):
    blocks = [slice(j * GROUP_W, (j + 1) * GROUP_W) for j in range(BRANCH_W // GROUP_W)]
    return jnp.concatenate([_dot(x[:, c], w_ref[c, c]) for c in blocks], axis=1)


def _sigmoid(x):
    return 1.0 / (1.0 + jnp.exp(-x))


def _softplus(x):
    return jnp.maximum(x, 0.0) + jnp.log(1.0 + jnp.exp(-jnp.abs(x)))


def _gelu_tanh(x):
    return 0.5 * x * (1.0 + jnp.tanh(0.7978845608028654 * (x + 0.044715 * (x * x * x))))


def _layer_norm(z, g, b, eps):
    mu = jnp.mean(z, axis=-1, keepdims=True)
    zc = z - mu
    var = jnp.mean(zc * zc, axis=-1, keepdims=True)
    return zc * lax.rsqrt(var + eps) * g + b


def _const_spec(shape):
    nd = len(shape)
    return pl.BlockSpec(shape, lambda *_: (0,) * nd)


def _params(n_grid_axes, **kwargs):
    return pltpu.CompilerParams(
        dimension_semantics=("arbitrary",) * n_grid_axes, vmem_limit_bytes=VMEM_LIMIT, **kwargs)


def _swiglu_ln(x_parts, xb_ref, h_ref, w1_ref, w3_ref, w2_ref, g, b, emit):
    rb = x_parts[0].shape[0]
    for s, x in enumerate(x_parts):
        xb_ref[s * rb:(s + 1) * rb, :] = x.astype(BF16)
    for j in range(N_FF_CHUNKS):
        cols = slice(j * FF_CHUNK, (j + 1) * FF_CHUNK)
        xb = xb_ref[...]
        a = _dot(xb, w1_ref[:, cols])
        c = _dot(xb, w3_ref[:, cols])
        h_ref[:, cols] = ((a * _sigmoid(a)) * c).astype(BF16)
    for s, x in enumerate(x_parts):
        rows = slice(s * rb, (s + 1) * rb)
        y = _dot(h_ref[rows, :], w2_ref[...])
        emit(rows, _layer_norm(ALPHA * x + 0.5 * y, g, b, LN_EPS))


def _ffn_kernel(x_ref, w1_ref, w3_ref, w2_ref, g_ref, b_ref, o_ref, ob_ref, xb_ref, h_ref):
    rb = x_ref.shape[0] // FFN_ROW_SPLIT

    def emit(rows, out):
        o_ref[rows, :] = out
        ob_ref[rows, :] = out.astype(BF16)

    _swiglu_ln([x_ref[s * rb:(s + 1) * rb, :] for s in range(FFN_ROW_SPLIT)], xb_ref, h_ref,
               w1_ref, w3_ref, w2_ref, g_ref[...], b_ref[...], emit)


def _merge_ffn_kernel(x_ref, o_ref, part_ref, g0_ref, p0_ref, wout_ref, g1_ref, b1_ref,
                      w1_ref, w3_ref, w2_ref, g2_ref, b2_ref, out_ref, xb_ref, h_ref):
    rb = x_ref.shape[0] // MERGE_ROW_SPLIT
    x2 = []
    for s in range(MERGE_ROW_SPLIT):
        rows = slice(s * rb, (s + 1) * rb)
        m0 = _dot(o_ref[rows, :], p0_ref[...])
        merged = part_ref[rows, :].astype(F32) + g0_ref[rows, :].astype(F32) * m0
        y = _dot(merged.astype(BF16), wout_ref[...])
        x2.append(_layer_norm(ALPHA * x_ref[rows, :] + y, g1_ref[...], b1_ref[...], LN_EPS))

    def emit(rows, out):
        out_ref[rows, :] = out

    _swiglu_ln(x2, xb_ref, h_ref, w1_ref, w3_ref, w2_ref, g2_ref[...], b2_ref[...], emit)


def _rows_spec(tm, width):
    return pl.BlockSpec((tm, width), lambda i: (i, 0))


def _ffn_weight_specs():
    return [
        _const_spec((D_MODEL, D_FF)),
        _const_spec((D_MODEL, D_FF)),
        _const_spec((D_FF, D_MODEL)),
    ]


def _ffn_call(x, ffn_w, ln_g, ln_b):
    n_rows = x.shape[0]
    tm = min(FFN_TILE, n_rows)
    vec = _const_spec((1, D_MODEL))
    return pl.pallas_call(
        _ffn_kernel,
        grid=(n_rows // tm,),
        in_specs=[_rows_spec(tm, D_MODEL)] + _ffn_weight_specs() + [vec, vec],
        out_specs=[_rows_spec(tm, D_MODEL)] * 2,
        out_shape=[jax.ShapeDtypeStruct((n_rows, D_MODEL), F32),
                   jax.ShapeDtypeStruct((n_rows, D_MODEL), BF16)],
        scratch_shapes=[pltpu.VMEM((tm, D_MODEL), BF16), pltpu.VMEM((tm, D_FF), BF16)],
        compiler_params=_params(1),
        name="ffn",
    )(x, *ffn_w, ln_g, ln_b)


def _merge_ffn_call(x, o_rwkv, part, g0, p0, w_out, ln_g1, ln_b1, ffn_w, ln_g2, ln_b2):
    n_rows = x.shape[0]
    tm = min(MERGE_TILE, n_rows)
    vec = _const_spec((1, D_MODEL))
    return pl.pallas_call(
        _merge_ffn_kernel,
        grid=(n_rows // tm,),
        in_specs=[
            _rows_spec(tm, D_MODEL), _rows_spec(tm, BRANCH_W), _rows_spec(tm, D_MODEL),
            _rows_spec(tm, D_MODEL), _const_spec((BRANCH_W, D_MODEL)),
            _const_spec((D_MODEL, D_MODEL)), vec, vec,
        ] + _ffn_weight_specs() + [vec, vec],
        out_specs=_rows_spec(tm, D_MODEL),
        out_shape=jax.ShapeDtypeStruct((n_rows, D_MODEL), F32),
        scratch_shapes=[pltpu.VMEM((tm, D_MODEL), BF16), pltpu.VMEM((tm, D_FF), BF16)],
        compiler_params=_params(1),
        name="merge_ffn",
    )(x, o_rwkv, part, g0, p0, w_out, ln_g1, ln_b1, *ffn_w, ln_g2, ln_b2)


def _seq_spec(n_tiles, tm, width):
    return pl.BlockSpec((tm, width), lambda b, i: (b * n_tiles + i, 0))


N_RWKV_WEIGHTS = 14
N_VRES_INPUTS = 4
N_MIX_WEIGHTS = 19


def _rwkv_in_stages(x_ref, weights, vres, outs, zbuf_ref):
    (win_ref, mu_ref, w0_ref, w2_ref, a0_ref, a2_ref, g2_ref, kk_ref, ka_ref, rk_ref,
     gng_ref, gnb_ref, ones_ref, tri_ref) = weights
    r_out, k_out, v_out, ka_out, bb_out, cum_out, gg_out, c2_out = outs[:8]
    tm = x_ref.shape[0]
    base = SUBLANES

    @pl.when(pl.program_id(1) == 0)
    def _():
        zbuf_ref[0:base, :] = jnp.zeros((base, RWKV_COLS), F32)

    xb = x_ref[...]

    def project(cols):
        zbuf_ref[base:base + tm, cols] = _dot(xb, win_ref[:, cols])

    def shifted(cols):
        z = zbuf_ref[base:base + tm, cols]
        zp = zbuf_ref[base - 1:base - 1 + tm, cols]
        return z + mu_ref[:, cols] * (zp - z)

    project(slice(1536, 1792))
    yield
    project(slice(512, 1024))
    yield
    zwa = shifted(slice(1536, 1664))
    zg = shifted(slice(1664, 1792))
    pre_w = _dot(jnp.tanh(zwa).astype(BF16), w2_ref[...])
    pre_a = _dot(zwa.astype(BF16), a2_ref[...])
    g = _dot(_sigmoid(zg).astype(BF16), g2_ref[...])
    yield
    project(slice(0, 512))
    yield
    w_log = -_softplus(-(w0_ref[...] + pre_w)) - 0.5
    lw = -jnp.exp(w_log)
    a = _sigmoid(a0_ref[...] + pre_a)
    k = shifted(slice(512, 1024))
    kk = k * kk_ref[...]
    ss = _dot_group_diag((kk * kk).astype(BF16), ones_ref)
    yield
    project(slice(1024, 1536))
    yield
    tri = tri_ref[...]
    hi = lw.astype(BF16)
    rem = lw - hi.astype(F32)
    mid = rem.astype(BF16)
    lo = (rem - mid.astype(F32)).astype(BF16)
    for c in range(tm // WKV_CHUNK):
        rows = slice(c * WKV_CHUNK, (c + 1) * WKV_CHUNK)
        cum_out[rows, :] = _dot(tri, hi[rows]) + _dot(tri, mid[rows]) + _dot(tri, lo[rows])
    yield
    kk = kk * jnp.minimum(lax.rsqrt(ss), 1e12)
    k2 = k * (1.0 + (a - 1.0) * ka_ref[...])
    r = shifted(slice(0, 512))
    rk_sum = _dot_group_diag((r * k2 * rk_ref[...]).astype(BF16), ones_ref)
    yield
    r_out[...] = r.astype(r_out.dtype)
    k_out[...] = k2.astype(k_out.dtype)
    ka_out[...] = (kk * jnp.exp(-lw)).astype(ka_out.dtype)
    bb_out[...] = (kk * a).astype(bb_out.dtype)
    gg_out[...] = gng_ref[...] * g
    yield
    v = shifted(slice(1024, 1536))
    if vres is not None:
        vf_ref, v0_ref, v1_ref, v2_ref = vres
        lora = _dot(_dot(v.astype(BF16), v1_ref[...]).astype(BF16), v2_ref[...])
        v = v + (vf_ref[...] - v) * _sigmoid(v0_ref[...] + lora)
    else:
        outs[8][...] = v
    v_out[...] = v.astype(v_out.dtype)
    c2_out[...] = (gnb_ref[...] + rk_sum * v) * g
    zbuf_ref[base - 1:base, :] = zbuf_ref[base + tm - 1:base + tm, :]


def _mix_in_stages(x_ref, weights, part_out, g0_out, xbuf_ref, a_ref, b_ref, hcar_ref):
    (wgu_ref, wgv_ref, wlx_ref, wly_ref, wgate_ref, gateb_ref, lng_ref, lnb_ref, ws_ref, sb_ref,
     convw_ref, convb_ref, wa_ref, ba_ref, wx_ref, bx_ref, lam_ref, p1_ref, p2_ref) = weights
    tm = x_ref.shape[0]
    base = SUBLANES

    @pl.when(pl.program_id(1) == 0)
    def _():
        xbuf_ref[0:base, :] = jnp.zeros((base, BRANCH_W), F32)
        hcar_ref[...] = jnp.zeros_like(hcar_ref)

    xb = x_ref[...]
    gate_cols = [slice(j * GROUP_W, (j + 1) * GROUP_W) for j in range(D_MODEL // GROUP_W)]

    def gate(idx, cols):
        return _sigmoid(_dot(xb, wgate_ref[idx, :, cols]) + gateb_ref[idx:idx + 1, cols])

    xbuf_ref[base:base + tm, :] = _dot(xb, wlx_ref[...])
    yield
    zv = _dot(xb, wgv_ref[...])
    yield
    g1 = [gate(1, cols) for cols in gate_cols]
    yield

    xc = convb_ref[...]
    for j in range(CONV_WIDTH):
        off = base - (CONV_WIDTH - 1) + j
        xc = xc + xbuf_ref[off:off + tm, :] * convw_ref[j:j + 1, :]
    hist = CONV_WIDTH - 1
    xbuf_ref[base - hist:base, :] = xbuf_ref[base + tm - hist:base + tm, :]
    xcb = xc.astype(BF16)
    yield
    pre_r = _dot_group_diag(xcb, wa_ref)
    pre_i = _dot_group_diag(xcb, wx_ref)
    yield
    vvb = _layer_norm(_gelu_tanh(zv), lng_ref[...], lnb_ref[...], LN_EPS).astype(BF16)
    g2 = [gate(2, gate_cols[0]), gate(2, gate_cols[1])]
    yield
    zu = _dot(xb, wgu_ref[...])
    yield
    a_parts, b_parts = [], []
    for half in range(BRANCH_W // GROUP_W):
        hc = slice(half * GROUP_W, (half + 1) * GROUP_W)
        rg = _sigmoid(pre_r[:, hc] + ba_ref[:, hc])
        ig = _sigmoid(pre_i[:, hc] + bx_ref[:, hc])
        log_a = (-LRU_C) * rg * _softplus(-lam_ref[:, hc])
        th = jnp.tanh(log_a)
        a_parts.append(jnp.exp(log_a))
        b_parts.append(jnp.sqrt(-2.0 * th / (1.0 - th)) * (ig * xc[:, hc]))
        g2.append(gate(2, gate_cols[2 + half]))
        yield
    a = jnp.concatenate(a_parts, axis=1)
    bc = jnp.concatenate(b_parts, axis=1)

    n_chunks = tm // GMLP_CHUNK
    chunk = lambda c: slice(c * GMLP_CHUNK, (c + 1) * GMLP_CHUNK)
    s_groups = []
    for g in range(GMLP_GROUPS):
        gcols = slice(g * GMLP_GROUP_DIM, (g + 1) * GMLP_GROUP_DIM)
        v_side = jnp.concatenate([vvb[chunk(c), gcols] for c in range(n_chunks)], axis=1)
        s_groups.append(_dot(ws_ref[g], v_side))
    s = jnp.concatenate(
        [jnp.concatenate([sg[:, chunk(c)] for sg in s_groups], axis=1) for c in range(n_chunks)],
        axis=0)
    yield

    grouped = (tm // SUBLANES, SUBLANES, BRANCH_W)
    a = a.reshape(grouped)
    bc = bc.reshape(grouped)
    sub = lax.broadcasted_iota(jnp.int32, grouped, 1)
    for level, d in enumerate((1, 2, 4)):
        g0_out[:, gate_cols[level]] = gate(0, gate_cols[level]).astype(BF16)
        yield
        keep = sub >= d
        a_sh = pltpu.roll(a, d, axis=1)
        b_sh = pltpu.roll(bc, d, axis=1)
        bc = jnp.where(keep, bc + a * b_sh, bc)
        a = jnp.where(keep, a * a_sh, a)
        yield
    a_ref[...] = a.reshape(tm, BRANCH_W)
    b_ref[...] = bc.reshape(tm, BRANCH_W)
    yield
    zy = _dot(xb, wly_ref[...])
    o_gmlp = _gelu_tanh(zu) * (s + sb_ref[...])

    hc = hcar_ref[...]
    for q in range(tm // SUBLANES):
        rows = slice(q * SUBLANES, (q + 1) * SUBLANES)
        h = b_ref[rows, :] + a_ref[rows, :] * hc
        b_ref[rows, :] = h
        hc = jnp.broadcast_to(h[SUBLANES - 1:SUBLANES, :], (SUBLANES, BRANCH_W))
    hcar_ref[...] = hc
    yield
    m1 = _dot(o_gmlp.astype(BF16), p1_ref[...])
    yield
    o_lru = (_gelu_tanh(zy) * b_ref[...]).astype(BF16)
    g0_out[:, gate_cols[3]] = gate(0, gate_cols[3]).astype(BF16)
    yield
    for j, cols in enumerate(gate_cols):
        m2 = _dot(o_lru, p2_ref[:, cols])
        part_out[:, cols] = (g1[j] * m1[:, cols] + g2[j] * m2).astype(BF16)
        yield


def _run_stages(stages):
    for _ in stages:
        pass


def _rwkv_in_kernel(*refs, has_vres):
    refs = list(refs)
    x_ref = refs.pop(0)
    weights = [refs.pop(0) for _ in range(N_RWKV_WEIGHTS)]
    vres = [refs.pop(0) for _ in range(N_VRES_INPUTS)] if has_vres else None
    zbuf_ref = refs.pop()
    _run_stages(_rwkv_in_stages(x_ref, weights, vres, refs, zbuf_ref))


def _mix_in_kernel(*refs):
    refs = list(refs)
    x_ref = refs.pop(0)
    weights = [refs.pop(0) for _ in range(N_MIX_WEIGHTS)]
    _run_stages(_mix_in_stages(x_ref, weights, *refs))


def _rwkv_in_call(x, n_batch, seq, params, v_first):
    tm = min(SEQ_TILE, seq)
    n_tiles = seq // tm
    n_rows = n_batch * seq
    has_vres = v_first is not None
    row = lambda w: _seq_spec(n_tiles, tm, w)
    vec = _const_spec((1, BRANCH_W))
    lora = _const_spec((128, BRANCH_W))
    in_specs = [row(D_MODEL), _const_spec((D_MODEL, RWKV_COLS)), _const_spec((1, RWKV_COLS)),
                vec, lora, vec, lora, lora, vec, vec, vec, vec, vec,
                _const_spec((BRANCH_W, BRANCH_W)), _const_spec((WKV_CHUNK, WKV_CHUNK))]
    tri = jnp.tril(jnp.ones((WKV_CHUNK, WKV_CHUNK), BF16))
    args = [x] + list(params) + [tri]
    n_out = 9
    if has_vres:
        in_specs += [row(BRANCH_W), vec, _const_spec((BRANCH_W, 128)), lora]
        args += list(v_first)
        n_out = 8
    return pl.pallas_call(
        functools.partial(_rwkv_in_kernel, has_vres=has_vres),
        grid=(n_batch, n_tiles),
        in_specs=in_specs,
        out_specs=[row(BRANCH_W)] * n_out,
        out_shape=[jax.ShapeDtypeStruct((n_rows, BRANCH_W), F32)] * n_out,
        scratch_shapes=[pltpu.VMEM((tm + SUBLANES, RWKV_COLS), F32)],
        compiler_params=_params(2),
        name="rwkv_in_vres" if has_vres else "rwkv_in",
    )(*args)


def _mix_in_call(x, n_batch, seq, params):
    tm = min(SEQ_TILE, seq)
    n_tiles = seq // tm
    n_rows = n_batch * seq
    row = lambda w: _seq_spec(n_tiles, tm, w)
    vec = _const_spec((1, BRANCH_W))
    wcol = _const_spec((D_MODEL, BRANCH_W))
    sq = _const_spec((BRANCH_W, BRANCH_W))
    in_specs = [row(D_MODEL), wcol, wcol, wcol, wcol, _const_spec((3, D_MODEL, D_MODEL)),
                _const_spec((3, D_MODEL)), vec, vec,
                _const_spec((GMLP_GROUPS, GMLP_CHUNK, GMLP_CHUNK)),
                _const_spec((tm, BRANCH_W)), _const_spec((CONV_WIDTH, BRANCH_W)), vec,
                sq, vec, sq, vec, vec,
                _const_spec((BRANCH_W, D_MODEL)), _const_spec((BRANCH_W, D_MODEL))]
    return pl.pallas_call(
        _mix_in_kernel,
        grid=(n_batch, n_tiles),
        in_specs=in_specs,
        out_specs=[row(D_MODEL), row(D_MODEL)],
        out_shape=[jax.ShapeDtypeStruct((n_rows, D_MODEL), BF16)] * 2,
        scratch_shapes=[pltpu.VMEM((tm + SUBLANES, BRANCH_W), F32),
                        pltpu.VMEM((tm, BRANCH_W), F32), pltpu.VMEM((tm, BRANCH_W), F32),
                        pltpu.VMEM((SUBLANES, BRANCH_W), F32)],
        compiler_params=_params(2),
        name="mix_in",
    )(x, *params)


def _block_diag(y, mask):
    return jnp.concatenate([y] * HEADS_PER_GROUP, axis=0) * mask


def _wkv_kernel(r_ref, k_ref, v_ref, ka_ref, bb_ref, cum_ref, gg_ref, c2_ref,
                mstrict_ref, mincl_ref, eye_ref, mbd_ref, mbdf_ref, ones_ref,
                o_ref, state_ref):
    n_batch = r_ref.shape[0]
    C = WKV_CHUNK

    @pl.when(pl.program_id(0) == 0)
    def _():
        state_ref[...] = jnp.zeros_like(state_ref)

    m_strict, m_incl, eye = mstrict_ref[...], mincl_ref[...], eye_ref[...]
    mbd, mbdf = mbd_ref[...], mbdf_ref[...]

    cum = cum_ref[...]
    ref = cum[:, C // 2 - 1:C // 2, :]
    e_fwd = jnp.exp(cum - ref)
    e_bwd = jnp.exp(ref - cum)
    e_ref = jnp.exp(ref)
    e_end = jnp.exp(cum[:, C - 1:C, :] - ref)
    a_hat = (-ka_ref[...] * e_fwd).astype(BF16)
    r_hat = (r_ref[...] * e_fwd).astype(BF16)
    kt = (k_ref[...] * e_bwd).astype(BF16)
    bt = (bb_ref[...] * e_bwd).astype(BF16)
    vb = v_ref[...].astype(BF16)

    chains = [(b, g) for b in range(n_batch) for g in range(N_GROUPS)]
    cols = lambda g: slice(g * GROUP_W, (g + 1) * GROUP_W)
    pick = lambda arr: [arr[b][:, cols(g)] for b, g in chains]
    ktg, btg, vg = pick(kt), pick(bt), pick(vb)
    xg = [jnp.concatenate([ah, rh], axis=0) for ah, rh in zip(pick(a_hat), pick(r_hat))]

    sc = [_dot_nt(x, jnp.concatenate([_block_diag(b_, mbd), _block_diag(k_, mbd)], axis=0))
          for x, b_, k_ in zip(xg, btg, ktg)]
    s_sc = [state_ref[b, g] * e_ref[b][:, cols(g)] for b, g in chains]
    p = [_dot_nt(x, s.astype(BF16)) for x, s in zip(xg, s_sc)]
    l_mat = [s[0:C, 0:GROUP_W] * m_strict for s in sc]
    a_k = [jnp.concatenate([s[0:C, GROUP_W:] * m_strict, s[C:, GROUP_W:] * m_incl],
                           axis=0).astype(BF16) for s in sc]
    a_qb = [(s[C:, 0:GROUP_W] * m_incl).astype(BF16) for s in sc]
    av = [_dot(a, _block_diag(v_, mbd)) for a, v_ in zip(a_k, vg)]

    lb = [l.astype(BF16) for l in l_mat]
    l_pow = [_dot(l, _block_diag(l, mbd)) for l in lb]
    t_mat = [eye + l for l in l_mat]
    for _ in range(4):
        lpb = [l.astype(BF16) for l in l_pow]
        res = [_dot(jnp.concatenate([t.astype(BF16), l], axis=0), _block_diag(l, mbd))
               for t, l in zip(t_mat, lpb)]
        t_mat = [t + r_[0:C] for t, r_ in zip(t_mat, res)]
        l_pow = [r_[C:] for r_ in res]
    t_mat = [t + _dot(t.astype(BF16), _block_diag(l.astype(BF16), mbd))
             for t, l in zip(t_mat, l_pow)]
    ub = [_dot(t.astype(BF16), _block_diag((p_[0:C] + a_[0:C]).astype(BF16), mbd)).astype(BF16)
          for t, p_, a_ in zip(t_mat, p, av)]
    o_part = [p_[C:] + a_[C:] + _dot(q, _block_diag(u, mbd))
              for p_, a_, q, u in zip(p, av, a_qb, ub)]
    for (b, g), s, v_, u, k_, b_ in zip(chains, s_sc, vg, ub, ktg, btg):
        ds = _dot_tn(jnp.concatenate([v_, u], axis=0), jnp.concatenate([k_, b_], axis=0))
        state_ref[b, g] = (s + ds) * mbdf * e_end[b][:, cols(g)]

    o = jnp.concatenate(
        [jnp.concatenate(o_part[b * N_GROUPS:(b + 1) * N_GROUPS], axis=1) for b in range(n_batch)],
        axis=0)
    mu = _dot_group_diag(o.astype(BF16), ones_ref) * (1.0 / HEAD_DIM)
    oc = o - mu
    var = _dot_group_diag((oc * oc).astype(BF16), ones_ref) * (1.0 / HEAD_DIM)
    gg = gg_ref[...].reshape(n_batch * C, BRANCH_W)
    c2 = c2_ref[...].reshape(n_batch * C, BRANCH_W)
    out = oc * lax.rsqrt(var + GN_EPS) * gg + c2
    o_ref[...] = out.reshape(n_batch, C, BRANCH_W).astype(o_ref.dtype)


def _wkv_consts():
    C = WKV_CHUNK
    t = jnp.arange(C)[:, None]
    col = jnp.arange(GROUP_W)[None, :]
    s = col % C
    m_strict = (t > s).astype(F32)
    m_incl = (t >= s).astype(F32)
    eye = (t == s).astype(F32)
    rows = jnp.arange(GROUP_W)[:, None]
    mbd = (rows // C == col // HEAD_DIM)
    hh = jnp.arange(BRANCH_W) // HEAD_DIM
    ones = (hh[:, None] == hh[None, :]).astype(BF16)
    return m_strict, m_incl, eye, mbd.astype(BF16), mbd.astype(F32), ones


def _wkv_call(n_batch, seq, r, k, v, ka, bb, cum, gg, c2):
    C = WKV_CHUNK
    blk = pl.BlockSpec((n_batch, C, BRANCH_W), lambda i: (0, i, 0))
    consts = _wkv_consts()
    view = lambda a: a.reshape(n_batch, seq, BRANCH_W)
    out = pl.pallas_call(
        _wkv_kernel,
        grid=(seq // C,),
        in_specs=[blk] * 8 + [_const_spec(c.shape) for c in consts],
        out_specs=blk,
        out_shape=jax.ShapeDtypeStruct((n_batch, seq, BRANCH_W), BF16),
        scratch_shapes=[pltpu.VMEM((n_batch, N_GROUPS, GROUP_W, GROUP_W), F32)],
        compiler_params=_params(1),
        name="wkv",
    )(*[view(a) for a in (r, k, v, ka, bb, cum, gg, c2)], *consts)
    return out.reshape(n_batch * seq, BRANCH_W)


def _ffn_weights(w1, w3, w2):
    return w1.astype(BF16), w3.astype(BF16), w2.astype(BF16)


def _row(p):
    return p.reshape(1, -1).astype(F32)


def _pad_rows(w, top, total):
    out = jnp.zeros((total, w.shape[1]), w.dtype)
    return lax.dynamic_update_slice(out, w, (top, 0))


def _block_diag_heads(w):
    h, n, m = w.shape
    eye = jnp.eye(h, dtype=w.dtype)
    return (eye[:, None, :, None] * w[:, :, None, :]).reshape(h * n, h * m)


def kernel(x, ln_g, ln_b, ffn_w1, ffn_w3, ffn_w2, w_in, gate_b, p_branch, w_out, rwkv_mu, rwkv_w0, rwkv_w2, rwkv_a0, rwkv_a2, rwkv_g2, rwkv_k_k, rwkv_k_a, rwkv_r_k, rwkv_gn_g, rwkv_gn_b, rwkv_v0, rwkv_v1, rwkv_v2, gmlp_ln_g, gmlp_ln_b, gmlp_ws, gmlp_sb, lru_conv_w, lru_conv_b, lru_wa, lru_ba, lru_wx, lru_bx, lru_lam):
    n_batch, seq, d_model = x.shape
    assert d_model == D_MODEL
    seq_tile = min(SEQ_TILE, seq)
    assert seq % seq_tile == 0 and seq_tile % GMLP_CHUNK == 0 and seq_tile % WKV_CHUNK == 0
    assert (n_batch * seq) % min(FFN_TILE, n_batch * seq) == 0
    depth = ln_g.shape[0]
    hh = jnp.arange(BRANCH_W) // HEAD_DIM
    head_ones = (hh[:, None] == hh[None, :]).astype(BF16)
    causal = jnp.tril(jnp.ones((GMLP_CHUNK, GMLP_CHUNK), F32))

    cur = x.reshape(n_batch * seq, d_model)
    v_first = None
    for l in range(depth):
        cur, cur_b = _ffn_call(cur, _ffn_weights(ffn_w1[l, 0], ffn_w3[l, 0], ffn_w2[l, 0]),
                               _row(ln_g[l, 0]), _row(ln_b[l, 0]))

        win = w_in[l].astype(BF16)
        rwkv_params = (
            win[:, :RWKV_COLS], _row(rwkv_mu[l]), _row(rwkv_w0[l]),
            _pad_rows(rwkv_w2[l].astype(BF16), 0, 128), _row(rwkv_a0[l]),
            _pad_rows(rwkv_a2[l].astype(BF16), 64, 128), rwkv_g2[l].astype(BF16),
            _row(rwkv_k_k[l]), _row(rwkv_k_a[l]), _row(rwkv_r_k[l]),
            _row(rwkv_gn_g[l]), _row(rwkv_gn_b[l]), head_ones)
        if l == 0:
            vres = None
        else:
            v1p = jnp.zeros((BRANCH_W, 128), BF16).at[:, :rwkv_v1.shape[2]].set(
                rwkv_v1[l - 1].astype(BF16))
            vres = (v_first, _row(rwkv_v0[l - 1]), v1p,
                    _pad_rows(rwkv_v2[l - 1].astype(BF16), 0, 128))
        sb_tile = jnp.tile(jnp.repeat(gmlp_sb[l].T, GMLP_GROUP_DIM, axis=1),
                           (seq_tile // GMLP_CHUNK, 1))
        mix_params = (
            win[:, 1792:2304], win[:, 2304:2816], win[:, 2816:3328], win[:, 3328:3840],
            win[:, 3840:].reshape(D_MODEL, 3, D_MODEL).transpose(1, 0, 2), gate_b[l].astype(F32),
            _row(gmlp_ln_g[l]), _row(gmlp_ln_b[l]), (gmlp_ws[l] * causal).astype(BF16),
            sb_tile.astype(F32), lru_conv_w[l].astype(F32), _row(lru_conv_b[l]),
            _block_diag_heads(lru_wa[l]).astype(BF16), _row(lru_ba[l]),
            _block_diag_heads(lru_wx[l]).astype(BF16), _row(lru_bx[l]), _row(lru_lam[l]),
            p_branch[l, 1].astype(BF16), p_branch[l, 2].astype(BF16))
        outs = _rwkv_in_call(cur_b, n_batch, seq, rwkv_params, vres)
        if l == 0:
            v_first = outs[8]
        o_rwkv = _wkv_call(n_batch, seq, *outs[:8])
        part, g0 = _mix_in_call(cur_b, n_batch, seq, mix_params)

        cur = _merge_ffn_call(
            cur, o_rwkv, part, g0, p_branch[l, 0].astype(BF16), w_out[l].astype(BF16),
            _row(ln_g[l, 1]), _row(ln_b[l, 1]),
            _ffn_weights(ffn_w1[l, 1], ffn_w3[l, 1], ffn_w2[l, 1]),
            _row(ln_g[l, 2]), _row(ln_b[l, 2]))
    return cur.reshape(n_batch, seq, d_model)
```

```python
import functools

import jax
import jax.numpy as jnp
from jax import lax
from jax.experimental import pallas as pl
from jax.experimental.pallas import tpu as pltpu

F32 = jnp.float32
BF16 = jnp.bfloat16

D_MODEL = 1024
D_FF = 2816
FF_CHUNK = 256
N_FF_CHUNKS = D_FF // FF_CHUNK
BRANCH_W = 512
HEAD_DIM = 64
N_HEADS = BRANCH_W // HEAD_DIM
RWKV_COLS = 1792
GMLP_GROUPS = 4
GMLP_GROUP_DIM = 128
GMLP_CHUNK = 128
WKV_CHUNK = 64
HEADS_PER_GROUP = 4
GROUP_W = HEADS_PER_GROUP * HEAD_DIM
N_GROUPS = N_HEADS // HEADS_PER_GROUP
CONV_WIDTH = 4
SUBLANES = 8
LN_EPS = 1e-5
GN_EPS = 64e-5
LRU_C = 8.0
DEPTH = 2
ALPHA = (2 * DEPTH) ** 0.25
FFN_TILE = 1024
MERGE_TILE = 512
FFN_ROW_SPLIT = 4
MERGE_ROW_SPLIT = 2
SEQ_TILE = 512
VMEM_LIMIT = 56 * 1024 * 1024


def _dot(a, b):
    return jnp.dot(a, b, preferred_element_type=F32)


def _dot_nt(a, b):
    return lax.dot_general(a, b, (((1,), (1,)), ((), ())), preferred_element_type=F32)


def _dot_tn(a, b):
    return lax.dot_general(a, b, (((0,), (0,)), ((), ())), preferred_element_type=F32)


def _dot_group_diag(x, w_ref):
    blocks = [slice(j * GROUP_W, (j + 1) * GROUP_W) for j in range(BRANCH_W // GROUP_W)]
    return jnp.concatenate([_dot(x[:, c], w_ref[c, c]) for c in blocks], axis=1)


def _sigmoid(x):
    return 1.0 / (1.0 + jnp.exp(-x))


def _softplus(x):
    return jnp.maximum(x, 0.0) + jnp.log(1.0 + jnp.exp(-jnp.abs(x)))


def _gelu_tanh(x):
    return 0.5 * x * (1.0 + jnp.tanh(0.7978845608028654 * (x + 0.044715 * (x * x * x))))


def _layer_norm(z, g, b, eps):
    mu = jnp.mean(z, axis=-1, keepdims=True)
    zc = z - mu
    var = jnp.mean(zc * zc, axis=-1, keepdims=True)
    return zc * lax.rsqrt(var + eps) * g + b


def _const_spec(shape):
    nd = len(shape)
    return pl.BlockSpec(shape, lambda *_: (0,) * nd)


def _params(n_grid_axes, **kwargs):
    return pltpu.CompilerParams(
        dimension_semantics=("arbitrary",) * n_grid_axes, vmem_limit_bytes=VMEM_LIMIT, **kwargs)


def _swiglu_ln(x_parts, xb_ref, h_ref, w1_ref, w3_ref, w2_ref, g, b, emit):
    rb = x_parts[0].shape[0]
    for s, x in enumerate(x_parts):
        xb_ref[s * rb:(s + 1) * rb, :] = x.astype(BF16)
    for j in range(N_FF_CHUNKS):
        cols = slice(j * FF_CHUNK, (j + 1) * FF_CHUNK)
        xb = xb_ref[...]
        a = _dot(xb, w1_ref[:, cols])
        c = _dot(xb, w3_ref[:, cols])
        h_ref[:, cols] = ((a * _sigmoid(a)) * c).astype(BF16)
    for s, x in enumerate(x_parts):
        rows = slice(s * rb, (s + 1) * rb)
        y = _dot(h_ref[rows, :], w2_ref[...])
        emit(rows, _layer_norm(ALPHA * x + 0.5 * y, g, b, LN_EPS))


def _ffn_kernel(x_ref, w1_ref, w3_ref, w2_ref, g_ref, b_ref, o_ref, ob_ref, xb_ref, h_ref):
    rb = x_ref.shape[0] // FFN_ROW_SPLIT

    def emit(rows, out):
        o_ref[rows, :] = out
        ob_ref[rows, :] = out.astype(BF16)

    _swiglu_ln([x_ref[s * rb:(s + 1) * rb, :] for s in range(FFN_ROW_SPLIT)], xb_ref, h_ref,
               w1_ref, w3_ref, w2_ref, g_ref[...], b_ref[...], emit)


def _merge_ffn_kernel(x_ref, o_ref, part_ref, g0_ref, p0_ref, wout_ref, g1_ref, b1_ref,
                      w1_ref, w3_ref, w2_ref, g2_ref, b2_ref, out_ref, xb_ref, h_ref):
    rb = x_ref.shape[0] // MERGE_ROW_SPLIT
    x2 = []
    for s in range(MERGE_ROW_SPLIT):
        rows = slice(s * rb, (s + 1) * rb)
        m0 = _dot(o_ref[rows, :], p0_ref[...])
        merged = part_ref[rows, :].astype(F32) + g0_ref[rows, :].astype(F32) * m0
        y = _dot(merged.astype(BF16), wout_ref[...])
        x2.append(_layer_norm(ALPHA * x_ref[rows, :] + y, g1_ref[...], b1_ref[...], LN_EPS))

    def emit(rows, out):
        out_ref[rows, :] = out

    _swiglu_ln(x2, xb_ref, h_ref, w1_ref, w3_ref, w2_ref, g2_ref[...], b2_ref[...], emit)


def _rows_spec(tm, width):
    return pl.BlockSpec((tm, width), lambda i: (i, 0))


def _ffn_weight_specs():
    return [
        _const_spec((D_MODEL, D_FF)),
        _const_spec((D_MODEL, D_FF)),
        _const_spec((D_FF, D_MODEL)),
    ]


def _ffn_call(x, ffn_w, ln_g, ln_b):
    n_rows = x.shape[0]
    tm = min(FFN_TILE, n_rows)
    vec = _const_spec((1, D_MODEL))
    return pl.pallas_call(
        _ffn_kernel,
        grid=(n_rows // tm,),
        in_specs=[_rows_spec(tm, D_MODEL)] + _ffn_weight_specs() + [vec, vec],
        out_specs=[_rows_spec(tm, D_MODEL)] * 2,
        out_shape=[jax.ShapeDtypeStruct((n_rows, D_MODEL), F32),
                   jax.ShapeDtypeStruct((n_rows, D_MODEL), BF16)],
        scratch_shapes=[pltpu.VMEM((tm, D_MODEL), BF16), pltpu.VMEM((tm, D_FF), BF16)],
        compiler_params=_params(1),
        name="ffn",
    )(x, *ffn_w, ln_g, ln_b)


def _merge_ffn_call(x, o_rwkv, part, g0, p0, w_out, ln_g1, ln_b1, ffn_w, ln_g2, ln_b2):
    n_rows = x.shape[0]
    tm = min(MERGE_TILE, n_rows)
    vec = _const_spec((1, D_MODEL))
    return pl.pallas_call(
        _merge_ffn_kernel,
        grid=(n_rows // tm,),
        in_specs=[
            _rows_spec(tm, D_MODEL), _rows_spec(tm, BRANCH_W), _rows_spec(tm, D_MODEL),
            _rows_spec(tm, D_MODEL), _const_spec((BRANCH_W, D_MODEL)),
            _const_spec((D_MODEL, D_MODEL)), vec, vec,
        ] + _ffn_weight_specs() + [vec, vec],
        out_specs=_rows_spec(tm, D_MODEL),
        out_shape=jax.ShapeDtypeStruct((n_rows, D_MODEL), F32),
        scratch_shapes=[pltpu.VMEM((tm, D_MODEL), BF16), pltpu.VMEM((tm, D_FF), BF16)],
        compiler_params=_params(1),
        name="merge_ffn",
    )(x, o_rwkv, part, g0, p0, w_out, ln_g1, ln_b1, *ffn_w, ln_g2, ln_b2)


def _seq_spec(n_tiles, tm, width):
    return pl.BlockSpec((tm, width), lambda b, i: (b * n_tiles + i, 0))


N_RWKV_WEIGHTS = 14
N_VRES_INPUTS = 4
N_MIX_WEIGHTS = 19


def _rwkv_in_stages(x_ref, weights, vres, outs, zbuf_ref):
    (win_ref, mu_ref, w0_ref, w2_ref, a0_ref, a2_ref, g2_ref, kk_ref, ka_ref, rk_ref,
     gng_ref, gnb_ref, ones_ref, tri_ref) = weights
    r_out, k_out, v_out, ka_out, bb_out, cum_out, gg_out, c2_out = outs[:8]
    tm = x_ref.shape[0]
    base = SUBLANES

    @pl.when(pl.program_id(1) == 0)
    def _():
        zbuf_ref[0:base, :] = jnp.zeros((base, RWKV_COLS), F32)

    xb = x_ref[...]

    def project(cols):
        zbuf_ref[base:base + tm, cols] = _dot(xb, win_ref[:, cols])

    def shifted(cols):
        z = zbuf_ref[base:base + tm, cols]
        zp = zbuf_ref[base - 1:base - 1 + tm, cols]
        return z + mu_ref[:, cols] * (zp - z)

    project(slice(1536, 1792))
    yield
    project(slice(512, 1024))
    yield
    zwa = shifted(slice(1536, 1664))
    zg = shifted(slice(1664, 1792))
    pre_w = _dot(jnp.tanh(zwa).astype(BF16), w2_ref[...])
    pre_a = _dot(zwa.astype(BF16), a2_ref[...])
    g = _dot(_sigmoid(zg).astype(BF16), g2_ref[...])
    yield
    project(slice(0, 512))
    yield
    w_log = -_softplus(-(w0_ref[...] + pre_w)) - 0.5
    lw = -jnp.exp(w_log)
    a = _sigmoid(a0_ref[...] + pre_a)
    k = shifted(slice(512, 1024))
    kk = k * kk_ref[...]
    ss = _dot_group_diag((kk * kk).astype(BF16), ones_ref)
    yield
    project(slice(1024, 1536))
    yield
    tri = tri_ref[...]
    hi = lw.astype(BF16)
    rem = lw - hi.astype(F32)
    mid = rem.astype(BF16)
    lo = (rem - mid.astype(F32)).astype(BF16)
    for c in range(tm // WKV_CHUNK):
        rows = slice(c * WKV_CHUNK, (c + 1) * WKV_CHUNK)
        cum_out[rows, :] = _dot(tri, hi[rows]) + _dot(tri, mid[rows]) + _dot(tri, lo[rows])
    yield
    kk = kk * jnp.minimum(lax.rsqrt(ss), 1e12)
    k2 = k * (1.0 + (a - 1.0) * ka_ref[...])
    r = shifted(slice(0, 512))
    rk_sum = _dot_group_diag((r * k2 * rk_ref[...]).astype(BF16), ones_ref)
    yield
    r_out[...] = r.astype(r_out.dtype)
    k_out[...] = k2.astype(k_out.dtype)
    ka_out[...] = (kk * jnp.exp(-lw)).astype(ka_out.dtype)
    bb_out[...] = (kk * a).astype(bb_out.dtype)
    gg_out[...] = gng_ref[...] * g
    yield
    v = shifted(slice(1024, 1536))
    if vres is not None:
        vf_ref, v0_ref, v1_ref, v2_ref = vres
        lora = _dot(_dot(v.astype(BF16), v1_ref[...]).astype(BF16), v2_ref[...])
        v = v + (vf_ref[...] - v) * _sigmoid(v0_ref[...] + lora)
    else:
        outs[8][...] = v
    v_out[...] = v.astype(v_out.dtype)
    c2_out[...] = (gnb_ref[...] + rk_sum * v) * g
    zbuf_ref[base - 1:base, :] = zbuf_ref[base + tm - 1:base + tm, :]


def _mix_in_stages(x_ref, weights, part_out, g0_out, xbuf_ref, a_ref, b_ref, hcar_ref):
    (wgu_ref, wgv_ref, wlx_ref, wly_ref, wgate_ref, gateb_ref, lng_ref, lnb_ref, ws_ref, sb_ref,
     convw_ref, convb_ref, wa_ref, ba_ref, wx_ref, bx_ref, lam_ref, p1_ref, p2_ref) = weights
    tm = x_ref.shape[0]
    base = SUBLANES

    @pl.when(pl.program_id(1) == 0)
    def _():
        xbuf_ref[0:base, :] = jnp.zeros((base, BRANCH_W), F32)
        hcar_ref[...] = jnp.zeros_like(hcar_ref)

    xb = x_ref[...]
    gate_cols = [slice(j * GROUP_W, (j + 1) * GROUP_W) for j in range(D_MODEL // GROUP_W)]

    def gate(idx, cols):
        return _sigmoid(_dot(xb, wgate_ref[idx, :, cols]) + gateb_ref[idx:idx + 1, cols])

    xbuf_ref[base:base + tm, :] = _dot(xb, wlx_ref[...])
    yield
    zv = _dot(xb, wgv_ref[...])
    yield
    g1 = [gate(1, cols) for cols in gate_cols]
    yield

    xc = convb_ref[...]
    for j in range(CONV_WIDTH):
        off = base - (CONV_WIDTH - 1) + j
        xc = xc + xbuf_ref[off:off + tm, :] * convw_ref[j:j + 1, :]
    hist = CONV_WIDTH - 1
    xbuf_ref[base - hist:base, :] = xbuf_ref[base + tm - hist:base + tm, :]
    xcb = xc.astype(BF16)
    yield
    pre_r = _dot_group_diag(xcb, wa_ref)
    pre_i = _dot_group_diag(xcb, wx_ref)
    yield
    vvb = _layer_norm(_gelu_tanh(zv), lng_ref[...], lnb_ref[...], LN_EPS).astype(BF16)
    g2 = [gate(2, gate_cols[0]), gate(2, gate_cols[1])]
    yield
    zu = _dot(xb, wgu_ref[...])
    yield
    a_parts, b_parts = [], []
    for half in range(BRANCH_W // GROUP_W):
        hc = slice(half * GROUP_W, (half + 1) * GROUP_W)
        rg = _sigmoid(pre_r[:, hc] + ba_ref[:, hc])
        ig = _sigmoid(pre_i[:, hc] + bx_ref[:, hc])
        log_a = (-LRU_C) * rg * _softplus(-lam_ref[:, hc])
        th = jnp.tanh(log_a)
        a_parts.append(jnp.exp(log_a))
        b_parts.append(jnp.sqrt(-2.0 * th / (1.0 - th)) * (ig * xc[:, hc]))
        g2.append(gate(2, gate_cols[2 + half]))
        yield
    a = jnp.concatenate(a_parts, axis=1)
    bc = jnp.concatenate(b_parts, axis=1)

    n_chunks = tm // GMLP_CHUNK
    chunk = lambda c: slice(c * GMLP_CHUNK, (c + 1) * GMLP_CHUNK)
    s_groups = []
    for g in range(GMLP_GROUPS):
        gcols = slice(g * GMLP_GROUP_DIM, (g + 1) * GMLP_GROUP_DIM)
        v_side = jnp.concatenate([vvb[chunk(c), gcols] for c in range(n_chunks)], axis=1)
        s_groups.append(_dot(ws_ref[g], v_side))
    s = jnp.concatenate(
        [jnp.concatenate([sg[:, chunk(c)] for sg in s_groups], axis=1) for c in range(n_chunks)],
        axis=0)
    yield

    grouped = (tm // SUBLANES, SUBLANES, BRANCH_W)
    a = a.reshape(grouped)
    bc = bc.reshape(grouped)
    sub = lax.broadcasted_iota(jnp.int32, grouped, 1)
    for level, d in enumerate((1, 2, 4)):
        g0_out[:, gate_cols[level]] = gate(0, gate_cols[level]).astype(BF16)
        yield
        keep = sub >= d
        a_sh = pltpu.roll(a, d, axis=1)
        b_sh = pltpu.roll(bc, d, axis=1)
        bc = jnp.where(keep, bc + a * b_sh, bc)
        a = jnp.where(keep, a * a_sh, a)
        yield
    a_ref[...] = a.reshape(tm, BRANCH_W)
    b_ref[...] = bc.reshape(tm, BRANCH_W)
    yield
    zy = _dot(xb, wly_ref[...])
    o_gmlp = _gelu_tanh(zu) * (s + sb_ref[...])

    hc = hcar_ref[...]
    for q in range(tm // SUBLANES):
        rows = slice(q * SUBLANES, (q + 1) * SUBLANES)
        h = b_ref[rows, :] + a_ref[rows, :] * hc
        b_ref[rows, :] = h
        hc = jnp.broadcast_to(h[SUBLANES - 1:SUBLANES, :], (SUBLANES, BRANCH_W))
    hcar_ref[...] = hc
    yield
    m1 = _dot(o_gmlp.astype(BF16), p1_ref[...])
    yield
    o_lru = (_gelu_tanh(zy) * b_ref[...]).astype(BF16)
    g0_out[:, gate_cols[3]] = gate(0, gate_cols[3]).astype(BF16)
    yield
    for j, cols in enumerate(gate_cols):
        m2 = _dot(o_lru, p2_ref[:, cols])
        part_out[:, cols] = (g1[j] * m1[:, cols] + g2[j] * m2).astype(BF16)
        yield


def _run_stages(stages):
    for _ in stages:
        pass


def _rwkv_in_kernel(*refs, has_vres):
    refs = list(refs)
    x_ref = refs.pop(0)
    weights = [refs.pop(0) for _ in range(N_RWKV_WEIGHTS)]
    vres = [refs.pop(0) for _ in range(N_VRES_INPUTS)] if has_vres else None
    zbuf_ref = refs.pop()
    _run_stages(_rwkv_in_stages(x_ref, weights, vres, refs, zbuf_ref))


def _mix_in_kernel(*refs):
    refs = list(refs)
    x_ref = refs.pop(0)
    weights = [refs.pop(0) for _ in range(N_MIX_WEIGHTS)]
    _run_stages(_mix_in_stages(x_ref, weights, *refs))


def _rwkv_in_call(x, n_batch, seq, params, v_first):
    tm = min(SEQ_TILE, seq)
    n_tiles = seq // tm
    n_rows = n_batch * seq
    has_vres = v_first is not None
    row = lambda w: _seq_spec(n_tiles, tm, w)
    vec = _const_spec((1, BRANCH_W))
    lora = _const_spec((128, BRANCH_W))
    in_specs = [row(D_MODEL), _const_spec((D_MODEL, RWKV_COLS)), _const_spec((1, RWKV_COLS)),
                vec, lora, vec, lora, lora, vec, vec, vec, vec, vec,
                _const_spec((BRANCH_W, BRANCH_W)), _const_spec((WKV_CHUNK, WKV_CHUNK))]
    tri = jnp.tril(jnp.ones((WKV_CHUNK, WKV_CHUNK), BF16))
    args = [x] + list(params) + [tri]
    n_out = 9
    if has_vres:
        in_specs += [row(BRANCH_W), vec, _const_spec((BRANCH_W, 128)), lora]
        args += list(v_first)
        n_out = 8
    return pl.pallas_call(
        functools.partial(_rwkv_in_kernel, has_vres=has_vres),
        grid=(n_batch, n_tiles),
        in_specs=in_specs,
        out_specs=[row(BRANCH_W)] * n_out,
        out_shape=[jax.ShapeDtypeStruct((n_rows, BRANCH_W), BF16 if j < 5 else F32)
                   for j in range(n_out)],
        scratch_shapes=[pltpu.VMEM((tm + SUBLANES, RWKV_COLS), F32)],
        compiler_params=_params(2),
        name="rwkv_in_vres" if has_vres else "rwkv_in",
    )(*args)


def _mix_in_call(x, n_batch, seq, params):
    tm = min(SEQ_TILE, seq)
    n_tiles = seq // tm
    n_rows = n_batch * seq
    row = lambda w: _seq_spec(n_tiles, tm, w)
    vec = _const_spec((1, BRANCH_W))
    wcol = _const_spec((D_MODEL, BRANCH_W))
    sq = _const_spec((BRANCH_W, BRANCH_W))
    in_specs = [row(D_MODEL), wcol, wcol, wcol, wcol, _const_spec((3, D_MODEL, D_MODEL)),
                _const_spec((3, D_MODEL)), vec, vec,
                _const_spec((GMLP_GROUPS, GMLP_CHUNK, GMLP_CHUNK)),
                _const_spec((tm, BRANCH_W)), _const_spec((CONV_WIDTH, BRANCH_W)), vec,
                sq, vec, sq, vec, vec,
                _const_spec((BRANCH_W, D_MODEL)), _const_spec((BRANCH_W, D_MODEL))]
    return pl.pallas_call(
        _mix_in_kernel,
        grid=(n_batch, n_tiles),
        in_specs=in_specs,
        out_specs=[row(D_MODEL), row(D_MODEL)],
        out_shape=[jax.ShapeDtypeStruct((n_rows, D_MODEL), BF16)] * 2,
        scratch_shapes=[pltpu.VMEM((tm + SUBLANES, BRANCH_W), F32),
                        pltpu.VMEM((tm, BRANCH_W), F32), pltpu.VMEM((tm, BRANCH_W), F32),
                        pltpu.VMEM((SUBLANES, BRANCH_W), F32)],
        compiler_params=_params(2),
        name="mix_in",
    )(x, *params)


def _block_diag(y, mask):
    return jnp.concatenate([y] * HEADS_PER_GROUP, axis=0) * mask


def _wkv_kernel(r_ref, k_ref, v_ref, ka_ref, bb_ref, cum_ref, gg_ref, c2_ref,
                mstrict_ref, mincl_ref, eye_ref, mbd_ref, mbdf_ref, ones_ref,
                o_ref, state_ref):
    n_batch = r_ref.shape[0]
    C = WKV_CHUNK

    @pl.when(pl.program_id(0) == 0)
    def _():
        state_ref[...] = jnp.zeros_like(state_ref)

    m_strict, m_incl, eye = mstrict_ref[...], mincl_ref[...], eye_ref[...]
    mbd, mbdf = mbd_ref[...], mbdf_ref[...]

    cum = cum_ref[...]
    ref = cum[:, C // 2 - 1:C // 2, :]
    e_fwd = jnp.exp(cum - ref)
    e_bwd = jnp.exp(ref - cum)
    e_ref = jnp.exp(ref)
    e_end = jnp.exp(cum[:, C - 1:C, :] - ref)
    a_hat = (-ka_ref[...] * e_fwd).astype(BF16)
    r_hat = (r_ref[...] * e_fwd).astype(BF16)
    kt = (k_ref[...] * e_bwd).astype(BF16)
    bt = (bb_ref[...] * e_bwd).astype(BF16)
    vb = v_ref[...].astype(BF16)

    chains = [(b, g) for b in range(n_batch) for g in range(N_GROUPS)]
    cols = lambda g: slice(g * GROUP_W, (g + 1) * GROUP_W)
    pick = lambda arr: [arr[b][:, cols(g)] for b, g in chains]
    ktg, btg, vg = pick(kt), pick(bt), pick(vb)
    xg = [jnp.concatenate([ah, rh], axis=0) for ah, rh in zip(pick(a_hat), pick(r_hat))]

    sc = [_dot_nt(x, jnp.concatenate([_block_diag(b_, mbd), _block_diag(k_, mbd)], axis=0))
          for x, b_, k_ in zip(xg, btg, ktg)]
    s_sc = [state_ref[b, g] * e_ref[b][:, cols(g)] for b, g in chains]
    p = [_dot_nt(x, s.astype(BF16)) for x, s in zip(xg, s_sc)]
    l_mat = [s[0:C, 0:GROUP_W] * m_strict for s in sc]
    a_k = [jnp.concatenate([s[0:C, GROUP_W:] * m_strict, s[C:, GROUP_W:] * m_incl],
                           axis=0).astype(BF16) for s in sc]
    a_qb = [(s[C:, 0:GROUP_W] * m_incl).astype(BF16) for s in sc]
    av = [_dot(a, _block_diag(v_, mbd)) for a, v_ in zip(a_k, vg)]

    lb = [l.astype(BF16) for l in l_mat]
    l_pow = [_dot(l, _block_diag(l, mbd)) for l in lb]
    t_mat = [eye + l for l in l_mat]
    for _ in range(4):
        lpb = [l.astype(BF16) for l in l_pow]
        res = [_dot(jnp.concatenate([t.astype(BF16), l], axis=0), _block_diag(l, mbd))
               for t, l in zip(t_mat, lpb)]
        t_mat = [t + r_[0:C] for t, r_ in zip(t_mat, res)]
        l_pow = [r_[C:] for r_ in res]
    t_mat = [t + _dot(t.astype(BF16), _block_diag(l.astype(BF16), mbd))
             for t, l in zip(t_mat, l_pow)]
    ub = [_dot(t.astype(BF16), _block_diag((p_[0:C] + a_[0:C]).astype(BF16), mbd)).astype(BF16)
          for t, p_, a_ in zip(t_mat, p, av)]
    o_part = [p_[C:] + a_[C:] + _dot(q, _block_diag(u, mbd))
              for p_, a_, q, u in zip(p, av, a_qb, ub)]
    for (b, g), s, v_, u, k_, b_ in zip(chains, s_sc, vg, ub, ktg, btg):
        ds = _dot_tn(jnp.concatenate([v_, u], axis=0), jnp.concatenate([k_, b_], axis=0))
        state_ref[b, g] = (s + ds) * mbdf * e_end[b][:, cols(g)]

    o = jnp.concatenate(
        [jnp.concatenate(o_part[b * N_GROUPS:(b + 1) * N_GROUPS], axis=1) for b in range(n_batch)],
        axis=0)
    mu = _dot_group_diag(o.astype(BF16), ones_ref) * (1.0 / HEAD_DIM)
    oc = o - mu
    var = _dot_group_diag((oc * oc).astype(BF16), ones_ref) * (1.0 / HEAD_DIM)
    gg = gg_ref[...].reshape(n_batch * C, BRANCH_W)
    c2 = c2_ref[...].reshape(n_batch * C, BRANCH_W)
    out = oc * lax.rsqrt(var + GN_EPS) * gg + c2
    o_ref[...] = out.reshape(n_batch, C, BRANCH_W).astype(o_ref.dtype)


def _wkv_consts():
    C = WKV_CHUNK
    t = jnp.arange(C)[:, None]
    col = jnp.arange(GROUP_W)[None, :]
    s = col % C
    m_strict = (t > s).astype(F32)
    m_incl = (t >= s).astype(F32)
    eye = (t == s).astype(F32)
    rows = jnp.arange(GROUP_W)[:, None]
    mbd = (rows // C == col // HEAD_DIM)
    hh = jnp.arange(BRANCH_W) // HEAD_DIM
    ones = (hh[:, None] == hh[None, :]).astype(BF16)
    return m_strict, m_incl, eye, mbd.astype(BF16), mbd.astype(F32), ones


def _wkv_call(n_batch, seq, r, k, v, ka, bb, cum, gg, c2):
    C = WKV_CHUNK
    blk = pl.BlockSpec((n_batch, C, BRANCH_W), lambda i: (0, i, 0))
    consts = _wkv_consts()
    view = lambda a: a.reshape(n_batch, seq, BRANCH_W)
    out = pl.pallas_call(
        _wkv_kernel,
        grid=(seq // C,),
        in_specs=[blk] * 8 + [_const_spec(c.shape) for c in consts],
        out_specs=blk,
        out_shape=jax.ShapeDtypeStruct((n_batch, seq, BRANCH_W), BF16),
        scratch_shapes=[pltpu.VMEM((n_batch, N_GROUPS, GROUP_W, GROUP_W), F32)],
        compiler_params=_params(1),
        name="wkv",
    )(*[view(a) for a in (r, k, v, ka, bb, cum, gg, c2)], *consts)
    return out.reshape(n_batch * seq, BRANCH_W)


def _ffn_weights(w1, w3, w2):
    return w1.astype(BF16), w3.astype(BF16), w2.astype(BF16)


def _row(p):
    return p.reshape(1, -1).astype(F32)


def _pad_rows(w, top, total):
    out = jnp.zeros((total, w.shape[1]), w.dtype)
    return lax.dynamic_update_slice(out, w, (top, 0))


def _block_diag_heads(w):
    h, n, m = w.shape
    eye = jnp.eye(h, dtype=w.dtype)
    return (eye[:, None, :, None] * w[:, :, None, :]).reshape(h * n, h * m)


def kernel(x, ln_g, ln_b, ffn_w1, ffn_w3, ffn_w2, w_in, gate_b, p_branch, w_out, rwkv_mu, rwkv_w0, rwkv_w2, rwkv_a0, rwkv_a2, rwkv_g2, rwkv_k_k, rwkv_k_a, rwkv_r_k, rwkv_gn_g, rwkv_gn_b, rwkv_v0, rwkv_v1, rwkv_v2, gmlp_ln_g, gmlp_ln_b, gmlp_ws, gmlp_sb, lru_conv_w, lru_conv_b, lru_wa, lru_ba, lru_wx, lru_bx, lru_lam):
    n_batch, seq, d_model = x.shape
    assert d_model == D_MODEL
    seq_tile = min(SEQ_TILE, seq)
    assert seq % seq_tile == 0 and seq_tile % GMLP_CHUNK == 0 and seq_tile % WKV_CHUNK == 0
    assert (n_batch * seq) % min(FFN_TILE, n_batch * seq) == 0
    depth = ln_g.shape[0]
    hh = jnp.arange(BRANCH_W) // HEAD_DIM
    head_ones = (hh[:, None] == hh[None, :]).astype(BF16)
    causal = jnp.tril(jnp.ones((GMLP_CHUNK, GMLP_CHUNK), F32))

    cur = x.reshape(n_batch * seq, d_model)
    v_first = None
    for l in range(depth):
        cur, cur_b = _ffn_call(cur, _ffn_weights(ffn_w1[l, 0], ffn_w3[l, 0], ffn_w2[l, 0]),
                               _row(ln_g[l, 0]), _row(ln_b[l, 0]))

        win = w_in[l].astype(BF16)
        rwkv_params = (
            win[:, :RWKV_COLS], _row(rwkv_mu[l]), _row(rwkv_w0[l]),
            _pad_rows(rwkv_w2[l].astype(BF16), 0, 128), _row(rwkv_a0[l]),
            _pad_rows(rwkv_a2[l].astype(BF16), 64, 128), rwkv_g2[l].astype(BF16),
            _row(rwkv_k_k[l]), _row(rwkv_k_a[l]), _row(rwkv_r_k[l]),
            _row(rwkv_gn_g[l]), _row(rwkv_gn_b[l]), head_ones)
        if l == 0:
            vres = None
        else:
            v1p = jnp.zeros((BRANCH_W, 128), BF16).at[:, :rwkv_v1.shape[2]].set(
                rwkv_v1[l - 1].astype(BF16))
            vres = (v_first, _row(rwkv_v0[l - 1]), v1p,
                    _pad_rows(rwkv_v2[l - 1].astype(BF16), 0, 128))
        sb_tile = jnp.tile(jnp.repeat(gmlp_sb[l].T, GMLP_GROUP_DIM, axis=1),
                           (seq_tile // GMLP_CHUNK, 1))
        mix_params = (
            win[:, 1792:2304], win[:, 2304:2816], win[:, 2816:3328], win[:, 3328:3840],
            win[:, 3840:].reshape(D_MODEL, 3, D_MODEL).transpose(1, 0, 2), gate_b[l].astype(F32),
            _row(gmlp_ln_g[l]), _row(gmlp_ln_b[l]), (gmlp_ws[l] * causal).astype(BF16),
            sb_tile.astype(F32), lru_conv_w[l].astype(F32), _row(lru_conv_b[l]),
            _block_diag_heads(lru_wa[l]).astype(BF16), _row(lru_ba[l]),
            _block_diag_heads(lru_wx[l]).astype(BF16), _row(lru_bx[l]), _row(lru_lam[l]),
            p_branch[l, 1].astype(BF16), p_branch[l, 2].astype(BF16))
        outs = _rwkv_in_call(cur_b, n_batch, seq, rwkv_params, vres)
        if l == 0:
            v_first = outs[8]
        o_rwkv = _wkv_call(n_batch, seq, *outs[:8])
        part, g0 = _mix_in_call(cur_b, n_batch, seq, mix_params)

        cur = _merge_ffn_call(
            cur, o_rwkv, part, g0, p_branch[l, 0].astype(BF16), w_out[l].astype(BF16),
            _row(ln_g[l, 1]), _row(ln_b[l, 1]),
            _ffn_weights(ffn_w1[l, 1], ffn_w3[l, 1], ffn_w2[l, 1]),
            _row(ln_g[l, 2]), _row(ln_b[l, 2]))
    return cur.reshape(n_batch, seq, d_model)
```

```python
import functools

import jax
import jax.numpy as jnp
from jax import lax
from jax.experimental import pallas as pl
from jax.experimental.pallas import tpu as pltpu

F32 = jnp.float32
BF16 = jnp.bfloat16

D_MODEL = 1024
D_FF = 2816
FF_CHUNK = 256
N_FF_CHUNKS = D_FF // FF_CHUNK
BRANCH_W = 512
HEAD_DIM = 64
N_HEADS = BRANCH_W // HEAD_DIM
LORA_W = 128
COLS_R = slice(0, BRANCH_W)
COLS_K = slice(BRANCH_W, 2 * BRANCH_W)
COLS_V = slice(2 * BRANCH_W, 3 * BRANCH_W)
COLS_WA = slice(3 * BRANCH_W, 3 * BRANCH_W + LORA_W)
COLS_G = slice(3 * BRANCH_W + LORA_W, 3 * BRANCH_W + 2 * LORA_W)
RWKV_COLS = 3 * BRANCH_W + 2 * LORA_W
COLS_GU = slice(RWKV_COLS, RWKV_COLS + BRANCH_W)
COLS_GV = slice(RWKV_COLS + BRANCH_W, RWKV_COLS + 2 * BRANCH_W)
COLS_LX = slice(RWKV_COLS + 2 * BRANCH_W, RWKV_COLS + 3 * BRANCH_W)
COLS_LY = slice(RWKV_COLS + 3 * BRANCH_W, RWKV_COLS + 4 * BRANCH_W)
OFF_GATE = RWKV_COLS + 4 * BRANCH_W
GMLP_GROUPS = 4
GMLP_GROUP_DIM = 128
GMLP_CHUNK = 128
WKV_CHUNK = 64
HEADS_PER_GROUP = 4
GROUP_W = HEADS_PER_GROUP * HEAD_DIM
N_GROUPS = N_HEADS // HEADS_PER_GROUP
CONV_WIDTH = 4
SUBLANES = 8
LN_EPS = 1e-5
GN_EPS = 64e-5
LRU_C = 8.0
DEPTH = 2
ALPHA = (2 * DEPTH) ** 0.25
FFN_TILE = 1024
MERGE_TILE = 512
FFN_ROW_SPLIT = 4
MERGE_ROW_SPLIT = 2
SEQ_TILE = 512
V7X_VMEM_BYTES = 64 * 1024 * 1024
VMEM_LIMIT = V7X_VMEM_BYTES * 7 // 8


def _dot(a, b):
    return jnp.dot(a, b, preferred_element_type=F32)


def _dot_nt(a, b):
    return lax.dot_general(a, b, (((1,), (1,)), ((), ())), preferred_element_type=F32)


def _dot_tn(a, b):
    return lax.dot_general(a, b, (((0,), (0,)), ((), ())), preferred_element_type=F32)


def _dot_group_diag(x, w_ref):
    blocks = [slice(j * GROUP_W, (j + 1) * GROUP_W) for j in range(BRANCH_W // GROUP_W)]
    return jnp.concatenate([_dot(x[:, c], w_ref[c, c]) for c in blocks], axis=1)


def _sigmoid(x):
    return 1.0 / (1.0 + jnp.exp(-x))


def _softplus(x):
    return jnp.maximum(x, 0.0) + jnp.log(1.0 + jnp.exp(-jnp.abs(x)))


def _gelu_tanh(x):
    return 0.5 * x * (1.0 + jnp.tanh(0.7978845608028654 * (x + 0.044715 * (x * x * x))))


def _layer_norm(z, g, b, eps):
    mu = jnp.mean(z, axis=-1, keepdims=True)
    zc = z - mu
    var = jnp.mean(zc * zc, axis=-1, keepdims=True)
    return zc * lax.rsqrt(var + eps) * g + b


def _const_spec(shape):
    nd = len(shape)
    return pl.BlockSpec(shape, lambda *_: (0,) * nd)


def _params(n_grid_axes, **kwargs):
    return pltpu.CompilerParams(
        dimension_semantics=("arbitrary",) * n_grid_axes, vmem_limit_bytes=VMEM_LIMIT, **kwargs)


def _swiglu_ln(x_parts, xb_ref, h_ref, w1_ref, w3_ref, w2_ref, g, b, emit):
    rb = x_parts[0].shape[0]
    for s, x in enumerate(x_parts):
        xb_ref[s * rb:(s + 1) * rb, :] = x.astype(BF16)
    for j in range(N_FF_CHUNKS):
        cols = slice(j * FF_CHUNK, (j + 1) * FF_CHUNK)
        xb = xb_ref[...]
        a = _dot(xb, w1_ref[:, cols])
        c = _dot(xb, w3_ref[:, cols])
        h_ref[:, cols] = ((a * _sigmoid(a)) * c).astype(BF16)
    for s, x in enumerate(x_parts):
        rows = slice(s * rb, (s + 1) * rb)
        y = _dot(h_ref[rows, :], w2_ref[...])
        emit(rows, _layer_norm(ALPHA * x + 0.5 * y, g, b, LN_EPS))


def _ffn_kernel(x_ref, w1_ref, w3_ref, w2_ref, g_ref, b_ref, o_ref, ob_ref, xb_ref, h_ref):
    rb = x_ref.shape[0] // FFN_ROW_SPLIT

    def emit(rows, out):
        o_ref[rows, :] = out
        ob_ref[rows, :] = out.astype(BF16)

    _swiglu_ln([x_ref[s * rb:(s + 1) * rb, :] for s in range(FFN_ROW_SPLIT)], xb_ref, h_ref,
               w1_ref, w3_ref, w2_ref, g_ref[...], b_ref[...], emit)


def _merge_ffn_kernel(x_ref, o_ref, part_ref, g0_ref, p0_ref, wout_ref, g1_ref, b1_ref,
                      w1_ref, w3_ref, w2_ref, g2_ref, b2_ref, out_ref, xb_ref, h_ref):
    rb = x_ref.shape[0] // MERGE_ROW_SPLIT
    x2 = []
    for s in range(MERGE_ROW_SPLIT):
        rows = slice(s * rb, (s + 1) * rb)
        m0 = _dot(o_ref[rows, :], p0_ref[...])
        merged = part_ref[rows, :].astype(F32) + g0_ref[rows, :].astype(F32) * m0
        y = _dot(merged.astype(BF16), wout_ref[...])
        x2.append(_layer_norm(ALPHA * x_ref[rows, :] + y, g1_ref[...], b1_ref[...], LN_EPS))

    def emit(rows, out):
        out_ref[rows, :] = out

    _swiglu_ln(x2, xb_ref, h_ref, w1_ref, w3_ref, w2_ref, g2_ref[...], b2_ref[...], emit)


def _rows_spec(tm, width):
    return pl.BlockSpec((tm, width), lambda i: (i, 0))


def _ffn_weight_specs():
    return [
        _const_spec((D_MODEL, D_FF)),
        _const_spec((D_MODEL, D_FF)),
        _const_spec((D_FF, D_MODEL)),
    ]


def _ffn_call(x, ffn_w, ln_g, ln_b):
    n_rows = x.shape[0]
    tm = min(FFN_TILE, n_rows)
    vec = _const_spec((1, D_MODEL))
    return pl.pallas_call(
        _ffn_kernel,
        grid=(n_rows // tm,),
        in_specs=[_rows_spec(tm, D_MODEL)] + _ffn_weight_specs() + [vec, vec],
        out_specs=[_rows_spec(tm, D_MODEL)] * 2,
        out_shape=[jax.ShapeDtypeStruct((n_rows, D_MODEL), F32),
                   jax.ShapeDtypeStruct((n_rows, D_MODEL), BF16)],
        scratch_shapes=[pltpu.VMEM((tm, D_MODEL), BF16), pltpu.VMEM((tm, D_FF), BF16)],
        compiler_params=_params(1),
        name="ffn",
    )(x, *ffn_w, ln_g, ln_b)


def _merge_ffn_call(x, o_rwkv, part, g0, p0, w_out, ln_g1, ln_b1, ffn_w, ln_g2, ln_b2):
    n_rows = x.shape[0]
    tm = min(MERGE_TILE, n_rows)
    vec = _const_spec((1, D_MODEL))
    return pl.pallas_call(
        _merge_ffn_kernel,
        grid=(n_rows // tm,),
        in_specs=[
            _rows_spec(tm, D_MODEL), _rows_spec(tm, BRANCH_W), _rows_spec(tm, D_MODEL),
            _rows_spec(tm, D_MODEL), _const_spec((BRANCH_W, D_MODEL)),
            _const_spec((D_MODEL, D_MODEL)), vec, vec,
        ] + _ffn_weight_specs() + [vec, vec],
        out_specs=_rows_spec(tm, D_MODEL),
        out_shape=jax.ShapeDtypeStruct((n_rows, D_MODEL), F32),
        scratch_shapes=[pltpu.VMEM((tm, D_MODEL), BF16), pltpu.VMEM((tm, D_FF), BF16)],
        compiler_params=_params(1),
        name="merge_ffn",
    )(x, o_rwkv, part, g0, p0, w_out, ln_g1, ln_b1, *ffn_w, ln_g2, ln_b2)


def _seq_spec(n_tiles, tm, width):
    return pl.BlockSpec((tm, width), lambda b, i: (b * n_tiles + i, 0))


N_RWKV_WEIGHTS = 14
N_VRES_INPUTS = 4
N_MIX_WEIGHTS = 19


def _rwkv_in_stages(x_ref, weights, vres, outs, zbuf_ref):
    (win_ref, mu_ref, w0_ref, w2_ref, a0_ref, a2_ref, g2_ref, kk_ref, ka_ref, rk_ref,
     gng_ref, gnb_ref, ones_ref, tri_ref) = weights
    r_out, k_out, v_out, ka_out, bb_out, cum_out, gg_out, c2_out = outs[:8]
    tm = x_ref.shape[0]
    base = SUBLANES

    @pl.when(pl.program_id(1) == 0)
    def _():
        zbuf_ref[0:base, :] = jnp.zeros((base, RWKV_COLS), F32)

    xb = x_ref[...]

    def project(cols):
        zbuf_ref[base:base + tm, cols] = _dot(xb, win_ref[:, cols])

    def shifted(cols):
        z = zbuf_ref[base:base + tm, cols]
        zp = zbuf_ref[base - 1:base - 1 + tm, cols]
        return z + mu_ref[:, cols] * (zp - z)

    project(slice(COLS_WA.start, COLS_G.stop))
    yield
    project(COLS_K)
    yield
    zwa = shifted(COLS_WA)
    zg = shifted(COLS_G)
    pre_w = _dot(jnp.tanh(zwa).astype(BF16), w2_ref[...])
    pre_a = _dot(zwa.astype(BF16), a2_ref[...])
    g = _dot(_sigmoid(zg).astype(BF16), g2_ref[...])
    yield
    project(COLS_R)
    yield
    w_log = -_softplus(-(w0_ref[...] + pre_w)) - 0.5
    lw = -jnp.exp(w_log)
    a = _sigmoid(a0_ref[...] + pre_a)
    k = shifted(COLS_K)
    kk = k * kk_ref[...]
    ss = _dot_group_diag((kk * kk).astype(BF16), ones_ref)
    yield
    project(COLS_V)
    yield
    tri = tri_ref[...]
    hi = lw.astype(BF16)
    rem = lw - hi.astype(F32)
    mid = rem.astype(BF16)
    lo = (rem - mid.astype(F32)).astype(BF16)
    for c in range(tm // WKV_CHUNK):
        rows = slice(c * WKV_CHUNK, (c + 1) * WKV_CHUNK)
        cum_out[rows, :] = _dot(tri, hi[rows]) + _dot(tri, mid[rows]) + _dot(tri, lo[rows])
    yield
    kk = kk * jnp.minimum(lax.rsqrt(ss), 1e12)
    k2 = k * (1.0 + (a - 1.0) * ka_ref[...])
    r = shifted(COLS_R)
    rk_sum = _dot_group_diag((r * k2 * rk_ref[...]).astype(BF16), ones_ref)
    yield
    r_out[...] = r.astype(r_out.dtype)
    k_out[...] = k2.astype(k_out.dtype)
    ka_out[...] = (kk * jnp.exp(-lw)).astype(ka_out.dtype)
    bb_out[...] = (kk * a).astype(bb_out.dtype)
    gg_out[...] = gng_ref[...] * g
    yield
    v = shifted(COLS_V)
    if vres is not None:
        vf_ref, v0_ref, v1_ref, v2_ref = vres
        lora = _dot(_dot(v.astype(BF16), v1_ref[...]).astype(BF16), v2_ref[...])
        v = v + (vf_ref[...] - v) * _sigmoid(v0_ref[...] + lora)
    else:
        outs[8][...] = v
    v_out[...] = v.astype(v_out.dtype)
    c2_out[...] = (gnb_ref[...] + rk_sum * v) * g
    zbuf_ref[base - 1:base, :] = zbuf_ref[base + tm - 1:base + tm, :]


def _mix_in_stages(x_ref, weights, part_out, g0_out, xbuf_ref, a_ref, b_ref, hcar_ref):
    (wgu_ref, wgv_ref, wlx_ref, wly_ref, wgate_ref, gateb_ref, lng_ref, lnb_ref, ws_ref, sb_ref,
     convw_ref, convb_ref, wa_ref, ba_ref, wx_ref, bx_ref, lam_ref, p1_ref, p2_ref) = weights
    tm = x_ref.shape[0]
    base = SUBLANES

    @pl.when(pl.program_id(1) == 0)
    def _():
        xbuf_ref[0:base, :] = jnp.zeros((base, BRANCH_W), F32)
        hcar_ref[...] = jnp.zeros_like(hcar_ref)

    xb = x_ref[...]
    gate_cols = [slice(j * GROUP_W, (j + 1) * GROUP_W) for j in range(D_MODEL // GROUP_W)]

    def gate(idx, cols):
        return _sigmoid(_dot(xb, wgate_ref[idx, :, cols]) + gateb_ref[idx:idx + 1, cols])

    xbuf_ref[base:base + tm, :] = _dot(xb, wlx_ref[...])
    yield
    zv = _dot(xb, wgv_ref[...])
    yield
    g1 = [gate(1, cols) for cols in gate_cols]
    g2 = [gate(2, gate_cols[0]), gate(2, gate_cols[1])]
    yield

    xc = convb_ref[...]
    for j in range(CONV_WIDTH):
        off = base - (CONV_WIDTH - 1) + j
        xc = xc + xbuf_ref[off:off + tm, :] * convw_ref[j:j + 1, :]
    hist = CONV_WIDTH - 1
    xbuf_ref[base - hist:base, :] = xbuf_ref[base + tm - hist:base + tm, :]
    xcb = xc.astype(BF16)
    yield
    pre_r = _dot_group_diag(xcb, wa_ref)
    pre_i = _dot_group_diag(xcb, wx_ref)
    yield
    vvb = _layer_norm(_gelu_tanh(zv), lng_ref[...], lnb_ref[...], LN_EPS).astype(BF16)
    yield
    zu = _dot(xb, wgu_ref[...])
    yield
    a_parts, b_parts = [], []
    for half in range(BRANCH_W // GROUP_W):
        hc = slice(half * GROUP_W, (half + 1) * GROUP_W)
        rg = _sigmoid(pre_r[:, hc] + ba_ref[:, hc])
        ig = _sigmoid(pre_i[:, hc] + bx_ref[:, hc])
        log_a = (-LRU_C) * rg * _softplus(-lam_ref[:, hc])
        th = jnp.tanh(log_a)
        a_parts.append(jnp.exp(log_a))
        b_parts.append(jnp.sqrt(-2.0 * th / (1.0 - th)) * (ig * xc[:, hc]))
        g2.append(gate(2, gate_cols[2 + half]))
        yield
    a = jnp.concatenate(a_parts, axis=1)
    bc = jnp.concatenate(b_parts, axis=1)

    n_chunks = tm // GMLP_CHUNK
    chunk = lambda c: slice(c * GMLP_CHUNK, (c + 1) * GMLP_CHUNK)
    s_groups = []
    for g in range(GMLP_GROUPS):
        gcols = slice(g * GMLP_GROUP_DIM, (g + 1) * GMLP_GROUP_DIM)
        v_side = jnp.concatenate([vvb[chunk(c), gcols] for c in range(n_chunks)], axis=1)
        s_groups.append(_dot(ws_ref[g], v_side))
    s = jnp.concatenate(
        [jnp.concatenate([sg[:, chunk(c)] for sg in s_groups], axis=1) for c in range(n_chunks)],
        axis=0)
    yield

    grouped = (tm // SUBLANES, SUBLANES, BRANCH_W)
    a = a.reshape(grouped)
    bc = bc.reshape(grouped)
    sub = lax.broadcasted_iota(jnp.int32, grouped, 1)
    for level, d in enumerate((1, 2, 4)):
        g0_out[:, gate_cols[level]] = gate(0, gate_cols[level]).astype(BF16)
        yield
        keep = sub >= d
        a_sh = pltpu.roll(a, d, axis=1)
        b_sh = pltpu.roll(bc, d, axis=1)
        bc = jnp.where(keep, bc + a * b_sh, bc)
        a = jnp.where(keep, a * a_sh, a)
        yield
    a_ref[...] = a.reshape(tm, BRANCH_W)
    b_ref[...] = bc.reshape(tm, BRANCH_W)
    yield
    zy = _dot(xb, wly_ref[...])
    o_gmlp = _gelu_tanh(zu) * (s + sb_ref[...])

    hc = hcar_ref[...]
    for q in range(tm // SUBLANES):
        rows = slice(q * SUBLANES, (q + 1) * SUBLANES)
        h = b_ref[rows, :] + a_ref[rows, :] * hc
        b_ref[rows, :] = h
        hc = jnp.broadcast_to(h[SUBLANES - 1:SUBLANES, :], (SUBLANES, BRANCH_W))
    hcar_ref[...] = hc
    yield
    m1 = _dot(o_gmlp.astype(BF16), p1_ref[...])
    yield
    o_lru = (_gelu_tanh(zy) * b_ref[...]).astype(BF16)
    g0_out[:, gate_cols[3]] = gate(0, gate_cols[3]).astype(BF16)
    yield
    for j, cols in enumerate(gate_cols):
        m2 = _dot(o_lru, p2_ref[:, cols])
        part_out[:, cols] = (g1[j] * m1[:, cols] + g2[j] * m2).astype(BF16)
        yield


def _run_stages(stages):
    for _ in stages:
        pass


def _rwkv_in_kernel(*refs, has_vres):
    refs = list(refs)
    x_ref = refs.pop(0)
    weights = [refs.pop(0) for _ in range(N_RWKV_WEIGHTS)]
    vres = [refs.pop(0) for _ in range(N_VRES_INPUTS)] if has_vres else None
    zbuf_ref = refs.pop()
    _run_stages(_rwkv_in_stages(x_ref, weights, vres, refs, zbuf_ref))


def _mix_in_kernel(*refs):
    refs = list(refs)
    x_ref = refs.pop(0)
    weights = [refs.pop(0) for _ in range(N_MIX_WEIGHTS)]
    _run_stages(_mix_in_stages(x_ref, weights, *refs))


def _rwkv_in_call(x, n_batch, seq, params, v_first):
    tm = min(SEQ_TILE, seq)
    n_tiles = seq // tm
    n_rows = n_batch * seq
    has_vres = v_first is not None
    row = lambda w: _seq_spec(n_tiles, tm, w)
    vec = _const_spec((1, BRANCH_W))
    lora = _const_spec((LORA_W, BRANCH_W))
    in_specs = [row(D_MODEL), _const_spec((D_MODEL, RWKV_COLS)), _const_spec((1, RWKV_COLS)),
                vec, lora, vec, lora, lora, vec, vec, vec, vec, vec,
                _const_spec((BRANCH_W, BRANCH_W)), _const_spec((WKV_CHUNK, WKV_CHUNK))]
    tri = jnp.tril(jnp.ones((WKV_CHUNK, WKV_CHUNK), BF16))
    args = [x] + list(params) + [tri]
    n_out = 9
    if has_vres:
        in_specs += [row(BRANCH_W), vec, _const_spec((BRANCH_W, LORA_W)), lora]
        args += list(v_first)
        n_out = 8
    return pl.pallas_call(
        functools.partial(_rwkv_in_kernel, has_vres=has_vres),
        grid=(n_batch, n_tiles),
        in_specs=in_specs,
        out_specs=[row(BRANCH_W)] * n_out,
        out_shape=[jax.ShapeDtypeStruct((n_rows, BRANCH_W), BF16 if j < 5 else F32)
                   for j in range(n_out)],
        scratch_shapes=[pltpu.VMEM((tm + SUBLANES, RWKV_COLS), F32)],
        compiler_params=_params(2),
        name="rwkv_in_vres" if has_vres else "rwkv_in",
    )(*args)


def _mix_in_call(x, n_batch, seq, params):
    tm = min(SEQ_TILE, seq)
    n_tiles = seq // tm
    n_rows = n_batch * seq
    row = lambda w: _seq_spec(n_tiles, tm, w)
    vec = _const_spec((1, BRANCH_W))
    wcol = _const_spec((D_MODEL, BRANCH_W))
    sq = _const_spec((BRANCH_W, BRANCH_W))
    in_specs = [row(D_MODEL), wcol, wcol, wcol, wcol, _const_spec((3, D_MODEL, D_MODEL)),
                _const_spec((3, D_MODEL)), vec, vec,
                _const_spec((GMLP_GROUPS, GMLP_CHUNK, GMLP_CHUNK)),
                _const_spec((tm, BRANCH_W)), _const_spec((CONV_WIDTH, BRANCH_W)), vec,
                sq, vec, sq, vec, vec,
                _const_spec((BRANCH_W, D_MODEL)), _const_spec((BRANCH_W, D_MODEL))]
    return pl.pallas_call(
        _mix_in_kernel,
        grid=(n_batch, n_tiles),
        in_specs=in_specs,
        out_specs=[row(D_MODEL), row(D_MODEL)],
        out_shape=[jax.ShapeDtypeStruct((n_rows, D_MODEL), BF16)] * 2,
        scratch_shapes=[pltpu.VMEM((tm + SUBLANES, BRANCH_W), F32),
                        pltpu.VMEM((tm, BRANCH_W), F32), pltpu.VMEM((tm, BRANCH_W), F32),
                        pltpu.VMEM((SUBLANES, BRANCH_W), F32)],
        compiler_params=_params(2),
        name="mix_in",
    )(x, *params)


def _block_diag(y, mask):
    return jnp.concatenate([y] * HEADS_PER_GROUP, axis=0) * mask


def _wkv_kernel(r_ref, k_ref, v_ref, ka_ref, bb_ref, cum_ref, gg_ref, c2_ref,
                mstrict_ref, mincl_ref, eye_ref, mbd_ref, mbdf_ref, ones_ref,
                o_ref, state_ref):
    n_batch = r_ref.shape[0]
    C = WKV_CHUNK

    @pl.when(pl.program_id(0) == 0)
    def _():
        state_ref[...] = jnp.zeros_like(state_ref)

    m_strict, m_incl, eye = mstrict_ref[...], mincl_ref[...], eye_ref[...]
    mbd, mbdf = mbd_ref[...], mbdf_ref[...]

    cum = cum_ref[...]
    ref = cum[:, C // 2 - 1:C // 2, :]
    e_fwd = jnp.exp(cum - ref)
    e_bwd = jnp.exp(ref - cum)
    e_ref = jnp.exp(ref)
    e_end = jnp.exp(cum[:, C - 1:C, :] - ref)
    a_hat = (-ka_ref[...] * e_fwd).astype(BF16)
    r_hat = (r_ref[...] * e_fwd).astype(BF16)
    kt = (k_ref[...] * e_bwd).astype(BF16)
    bt = (bb_ref[...] * e_bwd).astype(BF16)
    vb = v_ref[...]

    chains = [(b, g) for b in range(n_batch) for g in range(N_GROUPS)]
    cols = lambda g: slice(g * GROUP_W, (g + 1) * GROUP_W)
    pick = lambda arr: [arr[b][:, cols(g)] for b, g in chains]
    ktg, btg, vg = pick(kt), pick(bt), pick(vb)
    xg = [jnp.concatenate([ah, rh], axis=0) for ah, rh in zip(pick(a_hat), pick(r_hat))]

    sc = [_dot_nt(x, jnp.concatenate([_block_diag(b_, mbd), _block_diag(k_, mbd)], axis=0))
          for x, b_, k_ in zip(xg, btg, ktg)]
    s_sc = [state_ref[b, g] * e_ref[b][:, cols(g)] for b, g in chains]
    p = [_dot_nt(x, s.astype(BF16)) for x, s in zip(xg, s_sc)]
    l_mat = [s[0:C, 0:GROUP_W] * m_strict for s in sc]
    a_k = [jnp.concatenate([s[0:C, GROUP_W:] * m_strict, s[C:, GROUP_W:] * m_incl],
                           axis=0).astype(BF16) for s in sc]
    a_qb = [(s[C:, 0:GROUP_W] * m_incl).astype(BF16) for s in sc]
    av = [_dot(a, _block_diag(v_, mbd)) for a, v_ in zip(a_k, vg)]

    lb = [l.astype(BF16) for l in l_mat]
    l_pow = [_dot(l, _block_diag(l, mbd)) for l in lb]
    t_mat = [eye + l for l in l_mat]
    for _ in range(4):
        lpb = [l.astype(BF16) for l in l_pow]
        res = [_dot(jnp.concatenate([t.astype(BF16), l], axis=0), _block_diag(l, mbd))
               for t, l in zip(t_mat, lpb)]
        t_mat = [t + r_[0:C] for t, r_ in zip(t_mat, res)]
        l_pow = [r_[C:] for r_ in res]
    t_mat = [t + _dot(t.astype(BF16), _block_diag(l.astype(BF16), mbd))
             for t, l in zip(t_mat, l_pow)]
    ub = [_dot(t.astype(BF16), _block_diag((p_[0:C] + a_[0:C]).astype(BF16), mbd)).astype(BF16)
          for t, p_, a_ in zip(t_mat, p, av)]
    o_part = [p_[C:] + a_[C:] + _dot(q, _block_diag(u, mbd))
              for p_, a_, q, u in zip(p, av, a_qb, ub)]
    for (b, g), s, v_, u, k_, b_ in zip(chains, s_sc, vg, ub, ktg, btg):
        ds = _dot_tn(jnp.concatenate([v_, u], axis=0), jnp.concatenate([k_, b_], axis=0))
        state_ref[b, g] = (s + ds) * mbdf * e_end[b][:, cols(g)]

    o = jnp.concatenate(
        [jnp.concatenate(o_part[b * N_GROUPS:(b + 1) * N_GROUPS], axis=1) for b in range(n_batch)],
        axis=0)
    mu = _dot_group_diag(o.astype(BF16), ones_ref) * (1.0 / HEAD_DIM)
    oc = o - mu
    var = _dot_group_diag((oc * oc).astype(BF16), ones_ref) * (1.0 / HEAD_DIM)
    gg = gg_ref[...].reshape(n_batch * C, BRANCH_W)
    c2 = c2_ref[...].reshape(n_batch * C, BRANCH_W)
    out = oc * lax.rsqrt(var + GN_EPS) * gg + c2
    o_ref[...] = out.reshape(n_batch, C, BRANCH_W).astype(o_ref.dtype)


def _wkv_consts():
    C = WKV_CHUNK
    t = jnp.arange(C)[:, None]
    col = jnp.arange(GROUP_W)[None, :]
    s = col % C
    m_strict = (t > s).astype(F32)
    m_incl = (t >= s).astype(F32)
    eye = (t == s).astype(F32)
    rows = jnp.arange(GROUP_W)[:, None]
    mbd = (rows // C == col // HEAD_DIM)
    hh = jnp.arange(BRANCH_W) // HEAD_DIM
    ones = (hh[:, None] == hh[None, :]).astype(BF16)
    return m_strict, m_incl, eye, mbd.astype(BF16), mbd.astype(F32), ones


def _wkv_call(n_batch, seq, r, k, v, ka, bb, cum, gg, c2):
    C = WKV_CHUNK
    blk = pl.BlockSpec((n_batch, C, BRANCH_W), lambda i: (0, i, 0))
    consts = _wkv_consts()
    view = lambda a: a.reshape(n_batch, seq, BRANCH_W)
    out = pl.pallas_call(
        _wkv_kernel,
        grid=(seq // C,),
        in_specs=[blk] * 8 + [_const_spec(c.shape) for c in consts],
        out_specs=blk,
        out_shape=jax.ShapeDtypeStruct((n_batch, seq, BRANCH_W), BF16),
        scratch_shapes=[pltpu.VMEM((n_batch, N_GROUPS, GROUP_W, GROUP_W), F32)],
        compiler_params=_params(1),
        name="wkv",
    )(*[view(a) for a in (r, k, v, ka, bb, cum, gg, c2)], *consts)
    return out.reshape(n_batch * seq, BRANCH_W)


def _ffn_weights(w1, w3, w2):
    return w1.astype(BF16), w3.astype(BF16), w2.astype(BF16)


def _row(p):
    return p.reshape(1, -1).astype(F32)


def _pad_rows(w, top, total):
    out = jnp.zeros((total, w.shape[1]), w.dtype)
    return lax.dynamic_update_slice(out, w, (top, 0))


def _block_diag_heads(w):
    h, n, m = w.shape
    eye = jnp.eye(h, dtype=w.dtype)
    return (eye[:, None, :, None] * w[:, :, None, :]).reshape(h * n, h * m)


def kernel(x, ln_g, ln_b, ffn_w1, ffn_w3, ffn_w2, w_in, gate_b, p_branch, w_out, rwkv_mu, rwkv_w0, rwkv_w2, rwkv_a0, rwkv_a2, rwkv_g2, rwkv_k_k, rwkv_k_a, rwkv_r_k, rwkv_gn_g, rwkv_gn_b, rwkv_v0, rwkv_v1, rwkv_v2, gmlp_ln_g, gmlp_ln_b, gmlp_ws, gmlp_sb, lru_conv_w, lru_conv_b, lru_wa, lru_ba, lru_wx, lru_bx, lru_lam):
    n_batch, seq, d_model = x.shape
    assert d_model == D_MODEL
    seq_tile = min(SEQ_TILE, seq)
    assert seq % seq_tile == 0 and seq_tile % GMLP_CHUNK == 0 and seq_tile % WKV_CHUNK == 0
    assert (n_batch * seq) % min(FFN_TILE, n_batch * seq) == 0
    depth = ln_g.shape[0]
    hh = jnp.arange(BRANCH_W) // HEAD_DIM
    head_ones = (hh[:, None] == hh[None, :]).astype(BF16)
    causal = jnp.tril(jnp.ones((GMLP_CHUNK, GMLP_CHUNK), F32))

    cur = x.reshape(n_batch * seq, d_model)
    v_first = None
    for l in range(depth):
        cur, cur_b = _ffn_call(cur, _ffn_weights(ffn_w1[l, 0], ffn_w3[l, 0], ffn_w2[l, 0]),
                               _row(ln_g[l, 0]), _row(ln_b[l, 0]))

        win = w_in[l].astype(BF16)
        rwkv_params = (
            win[:, :RWKV_COLS], _row(rwkv_mu[l]), _row(rwkv_w0[l]),
            _pad_rows(rwkv_w2[l].astype(BF16), 0, LORA_W), _row(rwkv_a0[l]),
            _pad_rows(rwkv_a2[l].astype(BF16), rwkv_w2.shape[1], LORA_W),
            rwkv_g2[l].astype(BF16),
            _row(rwkv_k_k[l]), _row(rwkv_k_a[l]), _row(rwkv_r_k[l]),
            _row(rwkv_gn_g[l]), _row(rwkv_gn_b[l]), head_ones)
        if l == 0:
            vres = None
        else:
            v1p = jnp.zeros((BRANCH_W, LORA_W), BF16).at[:, :rwkv_v1.shape[2]].set(
                rwkv_v1[l - 1].astype(BF16))
            vres = (v_first, _row(rwkv_v0[l - 1]), v1p,
                    _pad_rows(rwkv_v2[l - 1].astype(BF16), 0, LORA_W))
        sb_tile = jnp.tile(jnp.repeat(gmlp_sb[l].T, GMLP_GROUP_DIM, axis=1),
                           (seq_tile // GMLP_CHUNK, 1))
        mix_params = (
            win[:, COLS_GU], win[:, COLS_GV], win[:, COLS_LX], win[:, COLS_LY],
            win[:, OFF_GATE:].reshape(D_MODEL, 3, D_MODEL).transpose(1, 0, 2), gate_b[l].astype(F32),
            _row(gmlp_ln_g[l]), _row(gmlp_ln_b[l]), (gmlp_ws[l] * causal).astype(BF16),
            sb_tile.astype(F32), lru_conv_w[l].astype(F32), _row(lru_conv_b[l]),
            _block_diag_heads(lru_wa[l]).astype(BF16), _row(lru_ba[l]),
            _block_diag_heads(lru_wx[l]).astype(BF16), _row(lru_bx[l]), _row(lru_lam[l]),
            p_branch[l, 1].astype(BF16), p_branch[l, 2].astype(BF16))
        outs = _rwkv_in_call(cur_b, n_batch, seq, rwkv_params, vres)
        if l == 0:
            v_first = outs[8]
        o_rwkv = _wkv_call(n_batch, seq, *outs[:8])
        part, g0 = _mix_in_call(cur_b, n_batch, seq, mix_params)

        cur = _merge_ffn_call(
            cur, o_rwkv, part, g0, p_branch[l, 0].astype(BF16), w_out[l].astype(BF16),
            _row(ln_g[l, 1]), _row(ln_b[l, 1]),
            _ffn_weights(ffn_w1[l, 1], ffn_w3[l, 1], ffn_w2[l, 1]),
            _row(ln_g[l, 2]), _row(ln_b[l, 2]))
    return cur.reshape(n_batch, seq, d_model)
```

```python
import functools

import jax
import jax.numpy as jnp
from jax import lax
from jax.experimental import pallas as pl
from jax.experimental.pallas import tpu as pltpu

F32 = jnp.float32
BF16 = jnp.bfloat16

D_MODEL = 1024
D_FF = 2816
FF_CHUNK = 256
N_FF_CHUNKS = D_FF // FF_CHUNK
BRANCH_W = 512
HEAD_DIM = 64
N_HEADS = BRANCH_W // HEAD_DIM
LORA_W = 128
COLS_R = slice(0, BRANCH_W)
COLS_K = slice(BRANCH_W, 2 * BRANCH_W)
COLS_V = slice(2 * BRANCH_W, 3 * BRANCH_W)
COLS_WA = slice(3 * BRANCH_W, 3 * BRANCH_W + LORA_W)
COLS_G = slice(3 * BRANCH_W + LORA_W, 3 * BRANCH_W + 2 * LORA_W)
RWKV_COLS = 3 * BRANCH_W + 2 * LORA_W
COLS_GU = slice(RWKV_COLS, RWKV_COLS + BRANCH_W)
COLS_GV = slice(RWKV_COLS + BRANCH_W, RWKV_COLS + 2 * BRANCH_W)
COLS_LX = slice(RWKV_COLS + 2 * BRANCH_W, RWKV_COLS + 3 * BRANCH_W)
COLS_LY = slice(RWKV_COLS + 3 * BRANCH_W, RWKV_COLS + 4 * BRANCH_W)
OFF_GATE = RWKV_COLS + 4 * BRANCH_W
GMLP_GROUPS = 4
GMLP_GROUP_DIM = 128
GMLP_CHUNK = 128
WKV_CHUNK = 64
INV_BASE_BLOCK = 4
HEADS_PER_GROUP = 4
GROUP_W = HEADS_PER_GROUP * HEAD_DIM
N_GROUPS = N_HEADS // HEADS_PER_GROUP
CONV_WIDTH = 4
SUBLANES = 8
LN_EPS = 1e-5
GN_EPS = 64e-5
LRU_C = 8.0
DEPTH = 2
ALPHA = (2 * DEPTH) ** 0.25
FFN_TILE = 1024
MERGE_TILE = 512
FFN_ROW_SPLIT = 4
MERGE_ROW_SPLIT = 2
SEQ_TILE = 512
V7X_VMEM_BYTES = 64 * 1024 * 1024
VMEM_LIMIT = V7X_VMEM_BYTES * 7 // 8


def _dot(a, b):
    return jnp.dot(a, b, preferred_element_type=F32)


def _dot_nt(a, b):
    return lax.dot_general(a, b, (((1,), (1,)), ((), ())), preferred_element_type=F32)


def _dot_tn(a, b):
    return lax.dot_general(a, b, (((0,), (0,)), ((), ())), preferred_element_type=F32)


def _dot_group_diag(x, w_ref):
    blocks = [slice(j * GROUP_W, (j + 1) * GROUP_W) for j in range(BRANCH_W // GROUP_W)]
    return jnp.concatenate([_dot(x[:, c], w_ref[c, c]) for c in blocks], axis=1)


def _sigmoid(x):
    return 1.0 / (1.0 + jnp.exp(-x))


def _softplus(x):
    return jnp.maximum(x, 0.0) + jnp.log(1.0 + jnp.exp(-jnp.abs(x)))


def _gelu_tanh(x):
    return 0.5 * x * (1.0 + jnp.tanh(0.7978845608028654 * (x + 0.044715 * (x * x * x))))


def _layer_norm(z, g, b, eps):
    mu = jnp.mean(z, axis=-1, keepdims=True)
    zc = z - mu
    var = jnp.mean(zc * zc, axis=-1, keepdims=True)
    return zc * lax.rsqrt(var + eps) * g + b


def _const_spec(shape):
    nd = len(shape)
    return pl.BlockSpec(shape, lambda *_: (0,) * nd)


def _params(n_grid_axes, **kwargs):
    return pltpu.CompilerParams(
        dimension_semantics=("arbitrary",) * n_grid_axes, vmem_limit_bytes=VMEM_LIMIT, **kwargs)


def _swiglu_ln(x_parts, xb_ref, h_ref, w1_ref, w3_ref, w2_ref, g, b, emit):
    rb = x_parts[0].shape[0]
    for s, x in enumerate(x_parts):
        xb_ref[s * rb:(s + 1) * rb, :] = x.astype(BF16)
    for j in range(N_FF_CHUNKS):
        cols = slice(j * FF_CHUNK, (j + 1) * FF_CHUNK)
        xb = xb_ref[...]
        a = _dot(xb, w1_ref[:, cols])
        c = _dot(xb, w3_ref[:, cols])
        h_ref[:, cols] = ((a * _sigmoid(a)) * c).astype(BF16)
    for s, x in enumerate(x_parts):
        rows = slice(s * rb, (s + 1) * rb)
        y = _dot(h_ref[rows, :], w2_ref[...])
        emit(rows, _layer_norm(ALPHA * x + 0.5 * y, g, b, LN_EPS))


def _ffn_kernel(x_ref, w1_ref, w3_ref, w2_ref, g_ref, b_ref, o_ref, ob_ref, xb_ref, h_ref):
    rb = x_ref.shape[0] // FFN_ROW_SPLIT

    def emit(rows, out):
        o_ref[rows, :] = out
        ob_ref[rows, :] = out.astype(BF16)

    _swiglu_ln([x_ref[s * rb:(s + 1) * rb, :] for s in range(FFN_ROW_SPLIT)], xb_ref, h_ref,
               w1_ref, w3_ref, w2_ref, g_ref[...], b_ref[...], emit)


def _merge_ffn_kernel(x_ref, o_ref, part_ref, g0_ref, p0_ref, wout_ref, g1_ref, b1_ref,
                      w1_ref, w3_ref, w2_ref, g2_ref, b2_ref, out_ref, xb_ref, h_ref):
    rb = x_ref.shape[0] // MERGE_ROW_SPLIT
    x2 = []
    for s in range(MERGE_ROW_SPLIT):
        rows = slice(s * rb, (s + 1) * rb)
        m0 = _dot(o_ref[rows, :], p0_ref[...])
        merged = part_ref[rows, :].astype(F32) + g0_ref[rows, :].astype(F32) * m0
        y = _dot(merged.astype(BF16), wout_ref[...])
        x2.append(_layer_norm(ALPHA * x_ref[rows, :] + y, g1_ref[...], b1_ref[...], LN_EPS))

    def emit(rows, out):
        out_ref[rows, :] = out

    _swiglu_ln(x2, xb_ref, h_ref, w1_ref, w3_ref, w2_ref, g2_ref[...], b2_ref[...], emit)


def _rows_spec(tm, width):
    return pl.BlockSpec((tm, width), lambda i: (i, 0))


def _ffn_weight_specs():
    return [
        _const_spec((D_MODEL, D_FF)),
        _const_spec((D_MODEL, D_FF)),
        _const_spec((D_FF, D_MODEL)),
    ]


def _ffn_call(x, ffn_w, ln_g, ln_b):
    n_rows = x.shape[0]
    tm = min(FFN_TILE, n_rows)
    vec = _const_spec((1, D_MODEL))
    return pl.pallas_call(
        _ffn_kernel,
        grid=(n_rows // tm,),
        in_specs=[_rows_spec(tm, D_MODEL)] + _ffn_weight_specs() + [vec, vec],
        out_specs=[_rows_spec(tm, D_MODEL)] * 2,
        out_shape=[jax.ShapeDtypeStruct((n_rows, D_MODEL), F32),
                   jax.ShapeDtypeStruct((n_rows, D_MODEL), BF16)],
        scratch_shapes=[pltpu.VMEM((tm, D_MODEL), BF16), pltpu.VMEM((tm, D_FF), BF16)],
        compiler_params=_params(1),
        name="ffn",
    )(x, *ffn_w, ln_g, ln_b)


def _merge_ffn_call(x, o_rwkv, part, g0, p0, w_out, ln_g1, ln_b1, ffn_w, ln_g2, ln_b2):
    n_rows = x.shape[0]
    tm = min(MERGE_TILE, n_rows)
    vec = _const_spec((1, D_MODEL))
    return pl.pallas_call(
        _merge_ffn_kernel,
        grid=(n_rows // tm,),
        in_specs=[
            _rows_spec(tm, D_MODEL), _rows_spec(tm, BRANCH_W), _rows_spec(tm, D_MODEL),
            _rows_spec(tm, D_MODEL), _const_spec((BRANCH_W, D_MODEL)),
            _const_spec((D_MODEL, D_MODEL)), vec, vec,
        ] + _ffn_weight_specs() + [vec, vec],
        out_specs=_rows_spec(tm, D_MODEL),
        out_shape=jax.ShapeDtypeStruct((n_rows, D_MODEL), F32),
        scratch_shapes=[pltpu.VMEM((tm, D_MODEL), BF16), pltpu.VMEM((tm, D_FF), BF16)],
        compiler_params=_params(1),
        name="merge_ffn",
    )(x, o_rwkv, part, g0, p0, w_out, ln_g1, ln_b1, *ffn_w, ln_g2, ln_b2)


def _seq_spec(n_tiles, tm, width):
    return pl.BlockSpec((tm, width), lambda b, i: (b * n_tiles + i, 0))


N_RWKV_WEIGHTS = 14
N_VRES_INPUTS = 4
N_MIX_WEIGHTS = 19


def _rwkv_in_stages(x_ref, weights, vres, outs, zbuf_ref):
    (win_ref, mu_ref, w0_ref, w2_ref, a0_ref, a2_ref, g2_ref, kk_ref, ka_ref, rk_ref,
     gng_ref, gnb_ref, ones_ref, tri_ref) = weights
    r_out, k_out, v_out, ka_out, bb_out, cum_out, gg_out, c2_out = outs[:8]
    tm = x_ref.shape[0]
    base = SUBLANES

    @pl.when(pl.program_id(1) == 0)
    def _():
        zbuf_ref[0:base, :] = jnp.zeros((base, RWKV_COLS), F32)

    xb = x_ref[...]

    def project(cols):
        zbuf_ref[base:base + tm, cols] = _dot(xb, win_ref[:, cols])

    def shifted(cols):
        z = zbuf_ref[base:base + tm, cols]
        zp = zbuf_ref[base - 1:base - 1 + tm, cols]
        return z + mu_ref[:, cols] * (zp - z)

    project(slice(COLS_WA.start, COLS_G.stop))
    yield
    project(COLS_K)
    yield
    zwa = shifted(COLS_WA)
    zg = shifted(COLS_G)
    pre_w = _dot(jnp.tanh(zwa).astype(BF16), w2_ref[...])
    pre_a = _dot(zwa.astype(BF16), a2_ref[...])
    g = _dot(_sigmoid(zg).astype(BF16), g2_ref[...])
    yield
    project(COLS_R)
    yield
    w_log = -_softplus(-(w0_ref[...] + pre_w)) - 0.5
    lw = -jnp.exp(w_log)
    a = _sigmoid(a0_ref[...] + pre_a)
    k = shifted(COLS_K)
    kk = k * kk_ref[...]
    ss = _dot_group_diag((kk * kk).astype(BF16), ones_ref)
    yield
    project(COLS_V)
    yield
    tri = tri_ref[...]
    hi = lw.astype(BF16)
    rem = lw - hi.astype(F32)
    mid = rem.astype(BF16)
    lo = (rem - mid.astype(F32)).astype(BF16)
    for c in range(tm // WKV_CHUNK):
        rows = slice(c * WKV_CHUNK, (c + 1) * WKV_CHUNK)
        cum_out[rows, :] = _dot(tri, hi[rows]) + _dot(tri, mid[rows]) + _dot(tri, lo[rows])
    yield
    kk = kk * jnp.minimum(lax.rsqrt(ss), 1e12)
    k2 = k * (1.0 + (a - 1.0) * ka_ref[...])
    r = shifted(COLS_R)
    rk_sum = _dot_group_diag((r * k2 * rk_ref[...]).astype(BF16), ones_ref)
    yield
    r_out[...] = r.astype(r_out.dtype)
    k_out[...] = k2.astype(k_out.dtype)
    ka_out[...] = (kk * jnp.exp(-lw)).astype(ka_out.dtype)
    bb_out[...] = (kk * a).astype(bb_out.dtype)
    gg_out[...] = gng_ref[...] * g
    yield
    v = shifted(COLS_V)
    if vres is not None:
        vf_ref, v0_ref, v1_ref, v2_ref = vres
        lora = _dot(_dot(v.astype(BF16), v1_ref[...]).astype(BF16), v2_ref[...])
        v = v + (vf_ref[...] - v) * _sigmoid(v0_ref[...] + lora)
    else:
        outs[8][...] = v
    v_out[...] = v.astype(v_out.dtype)
    c2_out[...] = (gnb_ref[...] + rk_sum * v) * g
    zbuf_ref[base - 1:base, :] = zbuf_ref[base + tm - 1:base + tm, :]


def _mix_in_stages(x_ref, weights, part_out, g0_out, xbuf_ref, a_ref, b_ref, hcar_ref):
    (wgu_ref, wgv_ref, wlx_ref, wly_ref, wgate_ref, gateb_ref, lng_ref, lnb_ref, ws_ref, sb_ref,
     convw_ref, convb_ref, wa_ref, ba_ref, wx_ref, bx_ref, lam_ref, p1_ref, p2_ref) = weights
    tm = x_ref.shape[0]
    base = SUBLANES

    @pl.when(pl.program_id(1) == 0)
    def _():
        xbuf_ref[0:base, :] = jnp.zeros((base, BRANCH_W), F32)
        hcar_ref[...] = jnp.zeros_like(hcar_ref)

    xb = x_ref[...]
    gate_cols = [slice(j * GROUP_W, (j + 1) * GROUP_W) for j in range(D_MODEL // GROUP_W)]

    def gate(idx, cols):
        return _sigmoid(_dot(xb, wgate_ref[idx, :, cols]) + gateb_ref[idx:idx + 1, cols])

    xbuf_ref[base:base + tm, :] = _dot(xb, wlx_ref[...])
    yield
    zv = _dot(xb, wgv_ref[...])
    yield
    g1 = [gate(1, cols) for cols in gate_cols]
    g2 = [gate(2, gate_cols[0]), gate(2, gate_cols[1])]
    yield

    xc = convb_ref[...]
    for j in range(CONV_WIDTH):
        off = base - (CONV_WIDTH - 1) + j
        xc = xc + xbuf_ref[off:off + tm, :] * convw_ref[j:j + 1, :]
    hist = CONV_WIDTH - 1
    xbuf_ref[base - hist:base, :] = xbuf_ref[base + tm - hist:base + tm, :]
    xcb = xc.astype(BF16)
    yield
    pre_r = _dot_group_diag(xcb, wa_ref)
    pre_i = _dot_group_diag(xcb, wx_ref)
    yield
    vvb = _layer_norm(_gelu_tanh(zv), lng_ref[...], lnb_ref[...], LN_EPS).astype(BF16)
    yield
    zu = _dot(xb, wgu_ref[...])
    yield
    a_parts, b_parts = [], []
    for half in range(BRANCH_W // GROUP_W):
        hc = slice(half * GROUP_W, (half + 1) * GROUP_W)
        rg = _sigmoid(pre_r[:, hc] + ba_ref[:, hc])
        ig = _sigmoid(pre_i[:, hc] + bx_ref[:, hc])
        log_a = (-LRU_C) * rg * _softplus(-lam_ref[:, hc])
        th = jnp.tanh(log_a)
        a_parts.append(jnp.exp(log_a))
        b_parts.append(jnp.sqrt(-2.0 * th / (1.0 - th)) * (ig * xc[:, hc]))
        g2.append(gate(2, gate_cols[2 + half]))
        yield
    a = jnp.concatenate(a_parts, axis=1)
    bc = jnp.concatenate(b_parts, axis=1)

    n_chunks = tm // GMLP_CHUNK
    chunk = lambda c: slice(c * GMLP_CHUNK, (c + 1) * GMLP_CHUNK)
    s_groups = []
    for g in range(GMLP_GROUPS):
        gcols = slice(g * GMLP_GROUP_DIM, (g + 1) * GMLP_GROUP_DIM)
        v_side = jnp.concatenate([vvb[chunk(c), gcols] for c in range(n_chunks)], axis=1)
        s_groups.append(_dot(ws_ref[g], v_side))
    s = jnp.concatenate(
        [jnp.concatenate([sg[:, chunk(c)] for sg in s_groups], axis=1) for c in range(n_chunks)],
        axis=0)
    yield

    grouped = (tm // SUBLANES, SUBLANES, BRANCH_W)
    a = a.reshape(grouped)
    bc = bc.reshape(grouped)
    sub = lax.broadcasted_iota(jnp.int32, grouped, 1)
    for level, d in enumerate((1, 2, 4)):
        g0_out[:, gate_cols[level]] = gate(0, gate_cols[level]).astype(BF16)
        yield
        keep = sub >= d
        a_sh = pltpu.roll(a, d, axis=1)
        b_sh = pltpu.roll(bc, d, axis=1)
        bc = jnp.where(keep, bc + a * b_sh, bc)
        a = jnp.where(keep, a * a_sh, a)
        yield
    a_ref[...] = a.reshape(tm, BRANCH_W)
    b_ref[...] = bc.reshape(tm, BRANCH_W)
    yield
    zy = _dot(xb, wly_ref[...])
    o_gmlp = _gelu_tanh(zu) * (s + sb_ref[...])

    hc = hcar_ref[...]
    for q in range(tm // SUBLANES):
        rows = slice(q * SUBLANES, (q + 1) * SUBLANES)
        h = b_ref[rows, :] + a_ref[rows, :] * hc
        b_ref[rows, :] = h
        hc = jnp.broadcast_to(h[SUBLANES - 1:SUBLANES, :], (SUBLANES, BRANCH_W))
    hcar_ref[...] = hc
    yield
    m1 = _dot(o_gmlp.astype(BF16), p1_ref[...])
    yield
    o_lru = (_gelu_tanh(zy) * b_ref[...]).astype(BF16)
    g0_out[:, gate_cols[3]] = gate(0, gate_cols[3]).astype(BF16)
    yield
    for j, cols in enumerate(gate_cols):
        m2 = _dot(o_lru, p2_ref[:, cols])
        part_out[:, cols] = (g1[j] * m1[:, cols] + g2[j] * m2).astype(BF16)
        yield


def _run_stages(stages):
    for _ in stages:
        pass


def _rwkv_in_kernel(*refs, has_vres):
    refs = list(refs)
    x_ref = refs.pop(0)
    weights = [refs.pop(0) for _ in range(N_RWKV_WEIGHTS)]
    vres = [refs.pop(0) for _ in range(N_VRES_INPUTS)] if has_vres else None
    zbuf_ref = refs.pop()
    _run_stages(_rwkv_in_stages(x_ref, weights, vres, refs, zbuf_ref))


def _mix_in_kernel(*refs):
    refs = list(refs)
    x_ref = refs.pop(0)
    weights = [refs.pop(0) for _ in range(N_MIX_WEIGHTS)]
    _run_stages(_mix_in_stages(x_ref, weights, *refs))


def _rwkv_in_call(x, n_batch, seq, params, v_first):
    tm = min(SEQ_TILE, seq)
    n_tiles = seq // tm
    n_rows = n_batch * seq
    has_vres = v_first is not None
    row = lambda w: _seq_spec(n_tiles, tm, w)
    vec = _const_spec((1, BRANCH_W))
    lora = _const_spec((LORA_W, BRANCH_W))
    in_specs = [row(D_MODEL), _const_spec((D_MODEL, RWKV_COLS)), _const_spec((1, RWKV_COLS)),
                vec, lora, vec, lora, lora, vec, vec, vec, vec, vec,
                _const_spec((BRANCH_W, BRANCH_W)), _const_spec((WKV_CHUNK, WKV_CHUNK))]
    tri = jnp.tril(jnp.ones((WKV_CHUNK, WKV_CHUNK), BF16))
    args = [x] + list(params) + [tri]
    n_out = 9
    if has_vres:
        in_specs += [row(BRANCH_W), vec, _const_spec((BRANCH_W, LORA_W)), lora]
        args += list(v_first)
        n_out = 8
    return pl.pallas_call(
        functools.partial(_rwkv_in_kernel, has_vres=has_vres),
        grid=(n_batch, n_tiles),
        in_specs=in_specs,
        out_specs=[row(BRANCH_W)] * n_out,
        out_shape=[jax.ShapeDtypeStruct((n_rows, BRANCH_W), BF16 if j < 5 else F32)
                   for j in range(n_out)],
        scratch_shapes=[pltpu.VMEM((tm + SUBLANES, RWKV_COLS), F32)],
        compiler_params=_params(2),
        name="rwkv_in_vres" if has_vres else "rwkv_in",
    )(*args)


def _mix_in_call(x, n_batch, seq, params):
    tm = min(SEQ_TILE, seq)
    n_tiles = seq // tm
    n_rows = n_batch * seq
    row = lambda w: _seq_spec(n_tiles, tm, w)
    vec = _const_spec((1, BRANCH_W))
    wcol = _const_spec((D_MODEL, BRANCH_W))
    sq = _const_spec((BRANCH_W, BRANCH_W))
    in_specs = [row(D_MODEL), wcol, wcol, wcol, wcol, _const_spec((3, D_MODEL, D_MODEL)),
                _const_spec((3, D_MODEL)), vec, vec,
                _const_spec((GMLP_GROUPS, GMLP_CHUNK, GMLP_CHUNK)),
                _const_spec((tm, BRANCH_W)), _const_spec((CONV_WIDTH, BRANCH_W)), vec,
                sq, vec, sq, vec, vec,
                _const_spec((BRANCH_W, D_MODEL)), _const_spec((BRANCH_W, D_MODEL))]
    return pl.pallas_call(
        _mix_in_kernel,
        grid=(n_batch, n_tiles),
        in_specs=in_specs,
        out_specs=[row(D_MODEL), row(D_MODEL)],
        out_shape=[jax.ShapeDtypeStruct((n_rows, D_MODEL), BF16)] * 2,
        scratch_shapes=[pltpu.VMEM((tm + SUBLANES, BRANCH_W), F32),
                        pltpu.VMEM((tm, BRANCH_W), F32), pltpu.VMEM((tm, BRANCH_W), F32),
                        pltpu.VMEM((SUBLANES, BRANCH_W), F32)],
        compiler_params=_params(2),
        name="mix_in",
    )(x, *params)


def _block_diag(y, mask):
    return jnp.concatenate([y] * HEADS_PER_GROUP, axis=0) * mask


def _wkv_kernel(r_ref, k_ref, v_ref, ka_ref, bb_ref, cum_ref, gg_ref, c2_ref,
                mstrict_ref, mincl_ref, eye_ref, mbd_ref, mbdf_ref, ones_ref, mrec_ref,
                o_ref, state_ref):
    n_batch = r_ref.shape[0]
    C = WKV_CHUNK

    @pl.when(pl.program_id(0) == 0)
    def _():
        state_ref[...] = jnp.zeros_like(state_ref)

    m_strict, m_incl, eye = mstrict_ref[...], mincl_ref[...], eye_ref[...]
    mbd, mbdf = mbd_ref[...], mbdf_ref[...]

    cum = cum_ref[...]
    ref = cum[:, C // 2 - 1:C // 2, :]
    e_fwd = jnp.exp(cum - ref)
    e_bwd = jnp.exp(ref - cum)
    e_ref = jnp.exp(ref)
    e_end = jnp.exp(cum[:, C - 1:C, :] - ref)
    a_hat = (-ka_ref[...] * e_fwd).astype(BF16)
    r_hat = (r_ref[...] * e_fwd).astype(BF16)
    kt = (k_ref[...] * e_bwd).astype(BF16)
    bt = (bb_ref[...] * e_bwd).astype(BF16)
    vb = v_ref[...]

    chains = [(b, g) for b in range(n_batch) for g in range(N_GROUPS)]
    cols = lambda g: slice(g * GROUP_W, (g + 1) * GROUP_W)
    pick = lambda arr: [arr[b][:, cols(g)] for b, g in chains]
    ktg, btg, vg = pick(kt), pick(bt), pick(vb)
    xg = [jnp.concatenate([ah, rh], axis=0) for ah, rh in zip(pick(a_hat), pick(r_hat))]

    sc = [_dot_nt(x, jnp.concatenate([_block_diag(b_, mbd), _block_diag(k_, mbd)], axis=0))
          for x, b_, k_ in zip(xg, btg, ktg)]
    s_sc = [state_ref[b, g] * e_ref[b][:, cols(g)] for b, g in chains]
    p = [_dot_nt(x, s.astype(BF16)) for x, s in zip(xg, s_sc)]
    l_mat = [s[0:C, 0:GROUP_W] * m_strict for s in sc]
    a_k = [jnp.concatenate([s[0:C, GROUP_W:] * m_strict, s[C:, GROUP_W:] * m_incl],
                           axis=0).astype(BF16) for s in sc]
    a_qb = [(s[C:, 0:GROUP_W] * m_incl).astype(BF16) for s in sc]
    av = [_dot(a, _block_diag(v_, mbd)) for a, v_ in zip(a_k, vg)]

    d_blk = [l * mrec_ref[0] for l in l_mat]
    d_sq = [_dot(d.astype(BF16), _block_diag(d.astype(BF16), mbd)) for d in d_blk]
    t_mat = [eye + d for d in d_blk]
    t_mat = [t + _dot(t.astype(BF16), _block_diag(q.astype(BF16), mbd))
             for t, q in zip(t_mat, d_sq)]
    for level in range(1, mrec_ref.shape[0]):
        tb = [t.astype(BF16) for t in t_mat]
        x = [_dot((l * mrec_ref[level]).astype(BF16), _block_diag(t_, mbd))
             for l, t_ in zip(l_mat, tb)]
        t_mat = [t + _dot(t_, _block_diag(x_.astype(BF16), mbd))
                 for t, t_, x_ in zip(t_mat, tb, x)]
    ub = [_dot(t.astype(BF16), _block_diag((p_[0:C] + a_[0:C]).astype(BF16), mbd)).astype(BF16)
          for t, p_, a_ in zip(t_mat, p, av)]
    o_part = [p_[C:] + a_[C:] + _dot(q, _block_diag(u, mbd))
              for p_, a_, q, u in zip(p, av, a_qb, ub)]
    for (b, g), s, v_, u, k_, b_ in zip(chains, s_sc, vg, ub, ktg, btg):
        ds = _dot_tn(jnp.concatenate([v_, u], axis=0), jnp.concatenate([k_, b_], axis=0))
        state_ref[b, g] = (s + ds) * mbdf * e_end[b][:, cols(g)]

    o = jnp.concatenate(
        [jnp.concatenate(o_part[b * N_GROUPS:(b + 1) * N_GROUPS], axis=1) for b in range(n_batch)],
        axis=0)
    mu = _dot_group_diag(o.astype(BF16), ones_ref) * (1.0 / HEAD_DIM)
    oc = o - mu
    var = _dot_group_diag((oc * oc).astype(BF16), ones_ref) * (1.0 / HEAD_DIM)
    gg = gg_ref[...].reshape(n_batch * C, BRANCH_W)
    c2 = c2_ref[...].reshape(n_batch * C, BRANCH_W)
    out = oc * lax.rsqrt(var + GN_EPS) * gg + c2
    o_ref[...] = out.reshape(n_batch, C, BRANCH_W).astype(o_ref.dtype)


def _wkv_consts():
    C = WKV_CHUNK
    t = jnp.arange(C)[:, None]
    col = jnp.arange(GROUP_W)[None, :]
    s = col % C
    m_strict = (t > s).astype(F32)
    m_incl = (t >= s).astype(F32)
    eye = (t == s).astype(F32)
    rows = jnp.arange(GROUP_W)[:, None]
    mbd = (rows // C == col // HEAD_DIM)
    hh = jnp.arange(BRANCH_W) // HEAD_DIM
    ones = (hh[:, None] == hh[None, :]).astype(BF16)
    assert INV_BASE_BLOCK == 4
    m_rec = [(t > s) & (t // INV_BASE_BLOCK == s // INV_BASE_BLOCK)]
    b = INV_BASE_BLOCK
    while b < C:
        m_rec.append((t // (2 * b) == s // (2 * b)) & ((t // b) % 2 == 1) & ((s // b) % 2 == 0))
        b *= 2
    m_rec = jnp.stack(m_rec).astype(F32)
    return m_strict, m_incl, eye, mbd.astype(BF16), mbd.astype(F32), ones, m_rec


def _wkv_call(n_batch, seq, r, k, v, ka, bb, cum, gg, c2):
    C = WKV_CHUNK
    blk = pl.BlockSpec((n_batch, C, BRANCH_W), lambda i: (0, i, 0))
    consts = _wkv_consts()
    view = lambda a: a.reshape(n_batch, seq, BRANCH_W)
    out = pl.pallas_call(
        _wkv_kernel,
        grid=(seq // C,),
        in_specs=[blk] * 8 + [_const_spec(c.shape) for c in consts],
        out_specs=blk,
        out_shape=jax.ShapeDtypeStruct((n_batch, seq, BRANCH_W), BF16),
        scratch_shapes=[pltpu.VMEM((n_batch, N_GROUPS, GROUP_W, GROUP_W), F32)],
        compiler_params=_params(1),
        name="wkv",
    )(*[view(a) for a in (r, k, v, ka, bb, cum, gg, c2)], *consts)
    return out.reshape(n_batch * seq, BRANCH_W)


def _ffn_weights(w1, w3, w2):
    return w1.astype(BF16), w3.astype(BF16), w2.astype(BF16)


def _row(p):
    return p.reshape(1, -1).astype(F32)


def _pad_rows(w, top, total):
    out = jnp.zeros((total, w.shape[1]), w.dtype)
    return lax.dynamic_update_slice(out, w, (top, 0))


def _block_diag_heads(w):
    h, n, m = w.shape
    eye = jnp.eye(h, dtype=w.dtype)
    return (eye[:, None, :, None] * w[:, :, None, :]).reshape(h * n, h * m)


def kernel(x, ln_g, ln_b, ffn_w1, ffn_w3, ffn_w2, w_in, gate_b, p_branch, w_out, rwkv_mu, rwkv_w0, rwkv_w2, rwkv_a0, rwkv_a2, rwkv_g2, rwkv_k_k, rwkv_k_a, rwkv_r_k, rwkv_gn_g, rwkv_gn_b, rwkv_v0, rwkv_v1, rwkv_v2, gmlp_ln_g, gmlp_ln_b, gmlp_ws, gmlp_sb, lru_conv_w, lru_conv_b, lru_wa, lru_ba, lru_wx, lru_bx, lru_lam):
    n_batch, seq, d_model = x.shape
    assert d_model == D_MODEL
    seq_tile = min(SEQ_TILE, seq)
    assert seq % seq_tile == 0 and seq_tile % GMLP_CHUNK == 0 and seq_tile % WKV_CHUNK == 0
    assert (n_batch * seq) % min(FFN_TILE, n_batch * seq) == 0
    depth = ln_g.shape[0]
    hh = jnp.arange(BRANCH_W) // HEAD_DIM
    head_ones = (hh[:, None] == hh[None, :]).astype(BF16)
    causal = jnp.tril(jnp.ones((GMLP_CHUNK, GMLP_CHUNK), F32))

    cur = x.reshape(n_batch * seq, d_model)
    v_first = None
    for l in range(depth):
        cur, cur_b = _ffn_call(cur, _ffn_weights(ffn_w1[l, 0], ffn_w3[l, 0], ffn_w2[l, 0]),
                               _row(ln_g[l, 0]), _row(ln_b[l, 0]))

        win = w_in[l].astype(BF16)
        rwkv_params = (
            win[:, :RWKV_COLS], _row(rwkv_mu[l]), _row(rwkv_w0[l]),
            _pad_rows(rwkv_w2[l].astype(BF16), 0, LORA_W), _row(rwkv_a0[l]),
            _pad_rows(rwkv_a2[l].astype(BF16), rwkv_w2.shape[1], LORA_W),
            rwkv_g2[l].astype(BF16),
            _row(rwkv_k_k[l]), _row(rwkv_k_a[l]), _row(rwkv_r_k[l]),
            _row(rwkv_gn_g[l]), _row(rwkv_gn_b[l]), head_ones)
        if l == 0:
            vres = None
        else:
            v1p = jnp.zeros((BRANCH_W, LORA_W), BF16).at[:, :rwkv_v1.shape[2]].set(
                rwkv_v1[l - 1].astype(BF16))
            vres = (v_first, _row(rwkv_v0[l - 1]), v1p,
                    _pad_rows(rwkv_v2[l - 1].astype(BF16), 0, LORA_W))
        sb_tile = jnp.tile(jnp.repeat(gmlp_sb[l].T, GMLP_GROUP_DIM, axis=1),
                           (seq_tile // GMLP_CHUNK, 1))
        mix_params = (
            win[:, COLS_GU], win[:, COLS_GV], win[:, COLS_LX], win[:, COLS_LY],
            win[:, OFF_GATE:].reshape(D_MODEL, 3, D_MODEL).transpose(1, 0, 2), gate_b[l].astype(F32),
            _row(gmlp_ln_g[l]), _row(gmlp_ln_b[l]), (gmlp_ws[l] * causal).astype(BF16),
            sb_tile.astype(F32), lru_conv_w[l].astype(F32), _row(lru_conv_b[l]),
            _block_diag_heads(lru_wa[l]).astype(BF16), _row(lru_ba[l]),
            _block_diag_heads(lru_wx[l]).astype(BF16), _row(lru_bx[l]), _row(lru_lam[l]),
            p_branch[l, 1].astype(BF16), p_branch[l, 2].astype(BF16))
        outs = _rwkv_in_call(cur_b, n_batch, seq, rwkv_params, vres)
        if l == 0:
            v_first = outs[8]
        o_rwkv = _wkv_call(n_batch, seq, *outs[:8])
        part, g0 = _mix_in_call(cur_b, n_batch, seq, mix_params)

        cur = _merge_ffn_call(
            cur, o_rwkv, part, g0, p_branch[l, 0].astype(BF16), w_out[l].astype(BF16),
            _row(ln_g[l, 1]), _row(ln_b[l, 1]),
            _ffn_weights(ffn_w1[l, 1], ffn_w3[l, 1], ffn_w2[l, 1]),
            _row(ln_g[l, 2]), _row(ln_b[l, 2]))
    return cur.reshape(n_batch, seq, d_model)
```
